```python
import jax, jax.numpy as jnp
from jax import lax
import numpy as np

D_MODEL = 1024
BATCH = 4
SEQ = 8192
DEPTH = 4

CHUNK = 64

TM_HEAD = 64
TM_WIDTH = D_MODEL // 2
TM_HEADS = TM_WIDTH // TM_HEAD
TM_DECAY_RANK = 64
TM_AICL_RANK = 64
TM_LN_EPS = 64e-5
SC_WIDTH = D_MODEL // 2
SC_KERNEL = 3
CF_WIDTH = D_MODEL // 2
CF_KERNEL = 31
SSD_HEAD = 64
SSD_WIDTH = D_MODEL // 2
SSD_HEADS = SSD_WIDTH // SSD_HEAD
SSD_STATE = 128
SSD_GROUPS = 2
SSD_HPG = SSD_HEADS // SSD_GROUPS
SSD_CONV = 4
SSD_XBC = SSD_WIDTH + 2 * SSD_GROUPS * SSD_STATE

TM_COLS = 4 * TM_WIDTH + TM_DECAY_RANK + TM_AICL_RANK
SC_COLS = 4 * SC_WIDTH
EVEN_COLS = TM_COLS + SC_COLS
CF_COLS = 3 * CF_WIDTH
SSD_COLS = SSD_WIDTH + SSD_XBC + SSD_HEADS
ODD_COLS = CF_COLS + SSD_COLS

NORM_EPS = 1e-6

kernel_name = "hybrid_rwkv7_shortconv_conformer_ssd_trunk"


def rms_norm(x, g):
    xf = x.astype(jnp.float32)
    y = xf * lax.rsqrt(jnp.mean(xf * xf, axis=-1, keepdims=True) + NORM_EPS)
    return (y * g.astype(jnp.float32)).astype(x.dtype)


def layer_norm(x, g, b, eps=1e-5):
    xf = x.astype(jnp.float32)
    mu = jnp.mean(xf, axis=-1, keepdims=True)
    var = jnp.mean(jnp.square(xf - mu), axis=-1, keepdims=True)
    y = (xf - mu) * lax.rsqrt(var + eps)
    return (y * g.astype(jnp.float32) + b.astype(jnp.float32)).astype(x.dtype)


def causal_dwconv(u, w):
    k, ch = w.shape
    return lax.conv_general_dilated(
        u, w[:, None, :].astype(u.dtype), window_strides=(1,), padding=[(k - 1, 0)],
        dimension_numbers=("NWC", "WIO", "NWC"), feature_group_count=ch)


def wkv7_scan(r, w, k, v, a, b):
    bsz, _, h, n = r.shape

    def step(s, inp):
        rt, wt, kt, vt, at, bt = inp
        sa = jnp.einsum("bhij,bhj->bhi", s, at)
        s = s * wt[:, :, None, :] + sa[..., None] * bt[:, :, None, :] + vt[..., None] * kt[:, :, None, :]
        return s, jnp.einsum("bhij,bhj->bhi", s, rt)

    xs = tuple(jnp.moveaxis(t, 1, 0) for t in (r, w, k, v, a, b))
    s0 = jnp.zeros((bsz, h, n, n), jnp.float32)
    _, ys = lax.scan(step, s0, xs)
    return jnp.moveaxis(ys, 0, 1)


def rwkv7_mix(p, mu, w0, w2, a0, a2, k_k, k_a, r_k, lnx_g, lnx_b):
    bsz, t, _ = p.shape
    dt_out = p.dtype
    p_prev = jnp.pad(p, ((0, 0), (1, 0), (0, 0)))[:, :-1]
    p = p + (p_prev - p) * mu
    W = TM_WIDTH
    r, k, v, g, wd, ad = jnp.split(p, [W, 2 * W, 3 * W, 4 * W, 4 * W + TM_DECAY_RANK], axis=-1)
    w = -jax.nn.softplus(-(w0 + jnp.tanh(wd) @ w2)) - 0.5
    a = jax.nn.sigmoid(a0 + ad @ a2)
    heads = lambda z: z.astype(jnp.float32).reshape(bsz, t, TM_HEADS, TM_HEAD)
    r, w, k, v, a = heads(r), heads(w), heads(k), heads(v), heads(a)
    kk = k * k_k.astype(jnp.float32)
    kk = kk * lax.rsqrt(jnp.maximum(jnp.sum(kk * kk, axis=-1, keepdims=True), 1e-24))
    k = k * (1.0 + (a - 1.0) * k_a.astype(jnp.float32))
    decay = jnp.exp(-jnp.exp(w))
    y = wkv7_scan(r, decay, k, v, -kk, kk * a)
    mean = jnp.mean(y, axis=-1, keepdims=True)
    var = jnp.mean(jnp.square(y - mean), axis=-1, keepdims=True)
    y = ((y - mean) * lax.rsqrt(var + TM_LN_EPS)).reshape(bsz, t, W)
    y = y * lnx_g.astype(jnp.float32) + lnx_b.astype(jnp.float32)
    bonus = jnp.sum(r * k * r_k.astype(jnp.float32), axis=-1, keepdims=True) * v
    y = y + bonus.reshape(bsz, t, W)
    return (y * jax.nn.silu(g.astype(jnp.float32))).astype(dt_out)


def short_conv_mix(p, conv_w):
    b_gate, c_gate, h, g = jnp.split(p, 4, axis=-1)
    y = b_gate * causal_dwconv(c_gate * h, conv_w)
    return y * jax.nn.silu(g)


def conformer_conv_mix(p, conv_w, conv_b, ln_g, ln_b):
    val, glu_gate, g = jnp.split(p, 3, axis=-1)
    u = val * jax.nn.sigmoid(glu_gate)
    u = causal_dwconv(u, conv_w) + conv_b
    u = jax.nn.silu(layer_norm(u, ln_g, ln_b))
    return u * jax.nn.silu(g)


def mamba2_ssd_mix(p, conv_w, conv_b, dt_bias, a_log, d_skip, norm_g):
    bsz, t, _ = p.shape
    dt_out = p.dtype
    nc = t // CHUNK
    G, E, P, N, L = SSD_GROUPS, SSD_HPG, SSD_HEAD, SSD_STATE, CHUNK
    z, xbc, dt = jnp.split(p, [SSD_WIDTH, SSD_WIDTH + SSD_XBC], axis=-1)
    xbc = jax.nn.silu(causal_dwconv(xbc, conv_w) + conv_b).astype(jnp.float32)
    xs, bm, cm = jnp.split(xbc, [SSD_WIDTH, SSD_WIDTH + G * N], axis=-1)
    dt = jax.nn.softplus(dt.astype(jnp.float32) + dt_bias.astype(jnp.float32))
    a = -jnp.exp(a_log.astype(jnp.float32))
    xh = xs.reshape(bsz, nc, L, G, E, P)
    dtc = dt.reshape(bsz, nc, L, G, E)
    x_dt = xh * dtc[..., None]
    da = jnp.transpose(dtc * a.reshape(G, E), (0, 3, 4, 1, 2))
    bm = bm.reshape(bsz, nc, L, G, N)
    cm = cm.reshape(bsz, nc, L, G, N)
    a_cs = jnp.cumsum(da, axis=-1)
    seg = a_cs[..., :, None] - a_cs[..., None, :]
    causal = jnp.tril(jnp.ones((L, L), dtype=bool))
    lmat = jnp.exp(jnp.where(causal, seg, -jnp.inf))
    cb = jnp.einsum("bclgn,bcsgn->bcgls", cm, bm)
    y_diag = jnp.einsum("bcgls,bgecls,bcsgep->bclgep", cb, lmat, x_dt)
    decay_states = jnp.exp(a_cs[..., -1:] - a_cs)
    states = jnp.einsum("bclgn,bgecl,bclgep->bcgepn", bm, decay_states, x_dt)
    chunk_decay = jnp.exp(a_cs[..., -1])

    def step(s, inp):
        st, dec = inp
        return s * dec[..., None, None] + st, s

    s0 = jnp.zeros((bsz, G, E, P, N), jnp.float32)
    _, prev = lax.scan(step, s0, (jnp.moveaxis(states, 1, 0), jnp.moveaxis(chunk_decay, -1, 0)))
    prev = jnp.moveaxis(prev, 0, 1)
    y_off = jnp.einsum("bclgn,bcgepn,bgecl->bclgep", cm, prev, jnp.exp(a_cs))
    y = y_diag + y_off + xh * d_skip.astype(jnp.float32).reshape(G, E, 1)
    y = y.reshape(bsz, t, SSD_WIDTH)
    y = rms_norm(y * jax.nn.silu(z.astype(jnp.float32)), norm_g)
    return y.astype(dt_out)


def setup_inputs(seed: int = 0) -> dict:
    key = jax.random.key(seed)
    ks = iter(jax.random.split(key, 40))
    nrm = lambda shape, s: jax.random.normal(next(ks), shape, jnp.float32) * s
    uni = lambda shape, lo, hi: jax.random.uniform(next(ks), shape, jnp.float32, lo, hi)
    D = D_MODEL
    NE, NO = (DEPTH + 1) // 2, DEPTH // 2
    dt_init = jnp.exp(uni((NO, SSD_HEADS), float(np.log(1e-3)), float(np.log(1e-1))))
    return {
        "x": nrm((BATCH, SEQ, D), 1.0),
        "c": nrm((BATCH, D), 1.0),
        "ada_w": nrm((DEPTH, D, 3 * D), 0.5 * D ** -0.5),
        "ada_b": nrm((DEPTH, 3 * D), 0.02),
        "norm_pre": 1.0 + nrm((DEPTH, D), 0.02),
        "norm_post": 1.0 + nrm((DEPTH, D), 0.02),
        "ev_w_in": nrm((NE, D, EVEN_COLS), D ** -0.5),
        "ev_w_out": nrm((NE, TM_WIDTH + SC_WIDTH, D), (TM_WIDTH + SC_WIDTH) ** -0.5),
        "tm_mu": uni((NE, TM_COLS), 0.0, 1.0),
        "tm_w0": uni((NE, TM_WIDTH), -6.0, -1.0),
        "tm_w2": nrm((NE, TM_DECAY_RANK, TM_WIDTH), TM_DECAY_RANK ** -0.5),
        "tm_a0": nrm((NE, TM_WIDTH), 0.1),
        "tm_a2": nrm((NE, TM_AICL_RANK, TM_WIDTH), TM_AICL_RANK ** -0.5),
        "tm_k_k": 0.85 + nrm((NE, TM_HEADS, TM_HEAD), 0.02),
        "tm_k_a": 1.0 + nrm((NE, TM_HEADS, TM_HEAD), 0.02),
        "tm_r_k": nrm((NE, TM_HEADS, TM_HEAD), 0.1),
        "tm_lnx_g": 1.0 + nrm((NE, TM_WIDTH), 0.02),
        "tm_lnx_b": nrm((NE, TM_WIDTH), 0.02),
        "sc_conv_w": nrm((NE, SC_KERNEL, SC_WIDTH), SC_KERNEL ** -0.5),
        "od_w_in": nrm((NO, D, ODD_COLS), D ** -0.5),
        "od_w_out": nrm((NO, CF_WIDTH + SSD_WIDTH, D), (CF_WIDTH + SSD_WIDTH) ** -0.5),
        "cf_conv_w": nrm((NO, CF_KERNEL, CF_WIDTH), CF_KERNEL ** -0.5),
        "cf_conv_b": nrm((NO, CF_WIDTH), 0.02),
        "cf_ln_g": 1.0 + nrm((NO, CF_WIDTH), 0.02),
        "cf_ln_b": nrm((NO, CF_WIDTH), 0.02),
        "ssd_conv_w": nrm((NO, SSD_CONV, SSD_XBC), SSD_CONV ** -0.5),
        "ssd_conv_b": nrm((NO, SSD_XBC), 0.02),
        "ssd_dt_bias": dt_init + jnp.log(-jnp.expm1(-dt_init)),
        "ssd_a_log": jnp.log(uni((NO, SSD_HEADS), 1.0, 16.0)),
        "ssd_d": 1.0 + nrm((NO, SSD_HEADS), 0.1),
        "ssd_norm_g": 1.0 + nrm((NO, SSD_WIDTH), 0.02),
    }


def reference(x, c, ada_w, ada_b, norm_pre, norm_post,
              ev_w_in, ev_w_out, tm_mu, tm_w0, tm_w2, tm_a0, tm_a2, tm_k_k, tm_k_a, tm_r_k,
              tm_lnx_g, tm_lnx_b, sc_conv_w,
              od_w_in, od_w_out, cf_conv_w, cf_conv_b, cf_ln_g, cf_ln_b,
              ssd_conv_w, ssd_conv_b, ssd_dt_bias, ssd_a_log, ssd_d, ssd_norm_g):
    c_act = jax.nn.silu(c)
    for i in range(DEPTH):
        shift, scale, gate = jnp.split(c_act @ ada_w[i] + ada_b[i], 3, axis=-1)
        h = rms_norm(x, norm_pre[i]) * (1.0 + scale[:, None, :]) + shift[:, None, :]
        j = i // 2
        if i % 2 == 0:
            p = h @ ev_w_in[j]
            y_a = rwkv7_mix(p[..., :TM_COLS], tm_mu[j], tm_w0[j], tm_w2[j], tm_a0[j], tm_a2[j],
                            tm_k_k[j], tm_k_a[j], tm_r_k[j], tm_lnx_g[j], tm_lnx_b[j])
            y_b = short_conv_mix(p[..., TM_COLS:], sc_conv_w[j])
            y = jnp.concatenate([y_a, y_b], axis=-1) @ ev_w_out[j]
        else:
            p = h @ od_w_in[j]
            y_c = conformer_conv_mix(p[..., :CF_COLS], cf_conv_w[j], cf_conv_b[j], cf_ln_g[j], cf_ln_b[j])
            y_d = mamba2_ssd_mix(p[..., CF_COLS:], ssd_conv_w[j], ssd_conv_b[j], ssd_dt_bias[j],
                                 ssd_a_log[j], ssd_d[j], ssd_norm_g[j])
            y = jnp.concatenate([y_c, y_d], axis=-1) @ od_w_out[j]
        x = x + gate[:, None, :] * rms_norm(y, norm_post[i])
    return x
```

```python
import functools

import jax
import jax.numpy as jnp
from jax import lax
from jax.experimental import pallas as pl
from jax.experimental.pallas import tpu as pltpu

F32 = jnp.float32
BF16 = jnp.bfloat16

D_MODEL = 1024
CHUNK = 64
HEAD = 64
HEADS = 8
WIDTH = 512
LORA = 64
TM_COLS = 4 * WIDTH + 2 * LORA
EVEN_COLS = TM_COLS + 4 * WIDTH
SSD_STATE = 128
SSD_GROUPS = 2
SSD_HPG = HEADS // SSD_GROUPS
SSD_XBC = WIDTH + 2 * SSD_GROUPS * SSD_STATE
CF_KERNEL = 31
CF_HALO = 32
LANES = 128
ODD_COLS_PAD = 3 * WIDTH + WIDTH + SSD_XBC + LANES
NORM_EPS = 1e-6
TM_LN_EPS = 64e-5
CF_LN_EPS = 1e-5
ROW_TILE = 256
VMEM_LIMIT = 48 * 1024 * 1024


def _dot(a, b):
    return jnp.dot(a.astype(BF16), b.astype(BF16), preferred_element_type=F32)


def _dot_nt(a, b):
    return lax.dot_general(a.astype(BF16), b.astype(BF16), (((1,), (1,)), ((), ())),
                           preferred_element_type=F32)


def _dot_tn(a, b):
    return lax.dot_general(a.astype(BF16), b.astype(BF16), (((0,), (0,)), ((), ())),
                           preferred_element_type=F32)


def _split_hi_lo(x):
    hi = x.astype(BF16)
    lo = (x - hi.astype(F32)).astype(BF16)
    return hi, lo


def _dot_exact_lhs(m, x):
    hi, lo = _split_hi_lo(x)
    mb = m.astype(BF16)
    return (jnp.dot(mb, hi, preferred_element_type=F32) +
            jnp.dot(mb, lo, preferred_element_type=F32))


def _sigmoid(x):
    return 1.0 / (1.0 + jnp.exp(-x))


def _silu(x):
    return x * _sigmoid(x)


def _softplus(x):
    return jnp.maximum(x, 0.0) + jnp.log(1.0 + jnp.exp(-jnp.abs(x)))


def _shift_rows(x, carry, d):
    n = x.shape[0]
    row = lax.broadcasted_iota(jnp.int32, (n, 1), 0)
    out = pltpu.roll(x, d, 0)
    for i in range(d):
        out = jnp.where(row == i, carry[8 - d + i:8 - d + i + 1, :], out)
    return out


def _ada_kernel(c_ref, w_ref, b_ref, o_ref):
    ca = _silu(c_ref[...])
    o_ref[...] = jnp.dot(ca, w_ref[...], preferred_element_type=F32,
                         precision=lax.Precision.HIGHEST) + b_ref[...]


def _ada_modulation(c, ada_w, ada_b):
    depth, d, d3 = ada_w.shape
    bsz = c.shape[0]
    rows = 8
    c_pad = jnp.zeros((rows, d), F32).at[:bsz].set(c)
    tn = 768
    out = pl.pallas_call(
        _ada_kernel,
        grid=(depth, d3 // tn),
        in_specs=[
            pl.BlockSpec((rows, d), lambda i, j: (0, 0)),
            pl.BlockSpec((None, d, tn), lambda i, j: (i, 0, j)),
            pl.BlockSpec((None, 1, tn), lambda i, j: (i, 0, j)),
        ],
        out_specs=pl.BlockSpec((None, rows, tn), lambda i, j: (i, 0, j)),
        out_shape=jax.ShapeDtypeStruct((depth, rows, d3), F32),
        compiler_params=pltpu.CompilerParams(
            dimension_semantics=("parallel", "parallel"), vmem_limit_bytes=VMEM_LIMIT),
        name="ada_modulation",
    )(c_pad, ada_w, ada_b.reshape(depth, 1, d3))
    return out[:, :bsz]


def _inproj_kernel(x_ref, g_ref, sc_ref, sh_ref, w_ref, o_ref):
    x = x_ref[...]
    ms = jnp.mean(x * x, axis=-1, keepdims=True)
    h = x * lax.rsqrt(ms + NORM_EPS) * g_ref[...]
    h = h * sc_ref[...] + sh_ref[...]
    o_ref[...] = _dot(h, w_ref[...])


def _in_projection(x, g, scale1, shift, w):
    bsz, t, d = x.shape
    cols = w.shape[1]
    return pl.pallas_call(
        _inproj_kernel,
        grid=(bsz, t // ROW_TILE),
        in_specs=[
            pl.BlockSpec((None, ROW_TILE, d), lambda b, i: (b, i, 0)),
            pl.BlockSpec((1, d), lambda b, i: (0, 0)),
            pl.BlockSpec((None, 1, d), lambda b, i: (b, 0, 0)),
            pl.BlockSpec((None, 1, d), lambda b, i: (b, 0, 0)),
            pl.BlockSpec((d, cols), lambda b, i: (0, 0)),
        ],
        out_specs=pl.BlockSpec((None, ROW_TILE, cols), lambda b, i: (b, i, 0)),
        out_shape=jax.ShapeDtypeStruct((bsz, t, cols), F32),
        compiler_params=pltpu.CompilerParams(
            dimension_semantics=("parallel", "parallel"), vmem_limit_bytes=VMEM_LIMIT),
        name="in_projection",
    )(x, g.reshape(1, d), scale1, shift, w)


def _outproj_kernel(y_ref, x_ref, g_ref, gate_ref, w_ref, o_ref):
    yo = _dot(y_ref[...], w_ref[...])
    ms = jnp.mean(yo * yo, axis=-1, keepdims=True)
    n = yo * lax.rsqrt(ms + NORM_EPS) * g_ref[...]
    o_ref[...] = x_ref[...] + gate_ref[...] * n


def _out_projection(y, x, g, gate, w):
    bsz, t, d = x.shape
    k = y.shape[-1]
    return pl.pallas_call(
        _outproj_kernel,
        grid=(bsz, t // ROW_TILE),
        in_specs=[
            pl.BlockSpec((None, ROW_TILE, k), lambda b, i: (b, i, 0)),
            pl.BlockSpec((None, ROW_TILE, d), lambda b, i: (b, i, 0)),
            pl.BlockSpec((1, d), lambda b, i: (0, 0)),
            pl.BlockSpec((None, 1, d), lambda b, i: (b, 0, 0)),
            pl.BlockSpec((k, d), lambda b, i: (0, 0)),
        ],
        out_specs=pl.BlockSpec((None, ROW_TILE, d), lambda b, i: (b, i, 0)),
        out_shape=jax.ShapeDtypeStruct((bsz, t, d), F32),
        compiler_params=pltpu.CompilerParams(
            dimension_semantics=("parallel", "parallel"), vmem_limit_bytes=VMEM_LIMIT),
        name="out_projection",
    )(y, x, g.reshape(1, d), gate, w)


def _even_mixer_kernel(p_ref, mu_ref, w0_ref, w2_ref, a0_ref, a2_ref, kk_ref, ka_ref, rk_ref,
                       lng_ref, lnb_ref, cw_ref, y_ref, prow_ref, ht_ref, uc_ref):
    L = CHUNK

    @pl.when(pl.program_id(1) == 0)
    def _():
        prow_ref[...] = jnp.zeros_like(prow_ref)
        ht_ref[...] = jnp.zeros_like(ht_ref)
        uc_ref[...] = jnp.zeros_like(uc_ref)

    p_tm = p_ref[:, :TM_COLS]
    row = lax.broadcasted_iota(jnp.int32, (L, 1), 0)
    prev = jnp.where(row == 0, prow_ref[...], pltpu.roll(p_tm, 1, 0))
    prow_ref[...] = p_tm[L - 1:L, :]
    ps = p_tm + (prev - p_tm) * mu_ref[...]
    W = WIDTH
    r = ps[:, 0:W]
    k = ps[:, W:2 * W]
    v = ps[:, 2 * W:3 * W]
    g = ps[:, 3 * W:4 * W]
    wd = ps[:, 4 * W:4 * W + LORA]
    ad = ps[:, 4 * W + LORA:4 * W + 2 * LORA]
    w_log = -_softplus(-(w0_ref[...] + _dot(jnp.tanh(wd), w2_ref[...]))) - 0.5
    lw = -jnp.exp(w_log)
    a_icl = _sigmoid(a0_ref[...] + _dot(ad, a2_ref[...]))
    kk_all = k * kk_ref[...]
    k2 = k * (1.0 + (a_icl - 1.0) * ka_ref[...])
    rk_prod = r * k2 * rk_ref[...]

    ii = lax.broadcasted_iota(jnp.int32, (L, L), 0)
    jj = lax.broadcasted_iota(jnp.int32, (L, L), 1)
    tri_incl = (ii >= jj).astype(F32)
    cum_incl = _dot_exact_lhs(tri_incl, lw)
    cum_excl = cum_incl - lw
    mid = cum_incl[L // 2 - 1:L // 2, :]
    cum_end = cum_incl[L - 1:L, :]
    e_a = jnp.exp(cum_excl - mid)
    e_r = jnp.exp(cum_incl - mid)
    e_k = jnp.exp(mid - cum_incl)
    e_end = jnp.exp(cum_end - cum_incl)
    g_mid = jnp.exp(mid)
    g_end = jnp.exp(cum_end)

    eye = (ii == jj).astype(F32)
    strict = ii > jj
    same4 = (ii >> 2) == (jj >> 2)
    same16 = (ii >> 4) == (jj >> 4)
    m0 = strict & same4
    m1 = strict & same16 & jnp.logical_not(same4)
    m2 = strict & jnp.logical_not(same16)
    i2 = lax.broadcasted_iota(jnp.int32, (2 * L, 2 * L), 0)
    j2 = lax.broadcasted_iota(jnp.int32, (2 * L, 2 * L), 1)
    li = i2 & (L - 1)
    lj = j2 & (L - 1)
    mask_aa = li >= jnp.where(i2 < L, lj + 1, lj)
    zero_ll = jnp.zeros((L, L), F32)

    outs = []
    for hd in range(HEADS):
        sl = slice(hd * HEAD, (hd + 1) * HEAD)
        kk = kk_all[:, sl]
        ssq = jnp.sum(kk * kk, axis=-1, keepdims=True)
        kk = kk * lax.rsqrt(jnp.maximum(ssq, 1e-24))
        a_h = a_icl[:, sl]
        v_h = v[:, sl]
        r_h = r[:, sl]
        k_h = k2[:, sl]
        at = -kk * e_a[:, sl]
        rt = r_h * e_r[:, sl]
        bt = kk * a_h * e_k[:, sl]
        kt = k_h * e_k[:, sl]
        bh = kk * a_h * e_end[:, sl]
        kh = k_h * e_end[:, sl]

        aa = _dot_nt(jnp.concatenate([at, rt], axis=0), jnp.concatenate([bt, kt], axis=0))
        aa = jnp.where(mask_aa, aa, 0.0)
        n_ab = aa[:L, :L]
        a_ak_v = _dot(aa[:L, L:], v_h)
        a_r = aa[L:, :]

        n0 = jnp.where(m0, n_ab, 0.0)
        d0 = _dot(eye + n0, eye + _dot(n0, n0))
        p1 = _dot(d0, jnp.where(m1, n_ab, 0.0))
        e1 = d0 + _dot(p1, d0)
        d1 = e1 + _dot(_dot(p1, p1), e1)
        p2 = _dot(d1, jnp.where(m2, n_ab, 0.0))
        z = _dot(d1, jnp.concatenate([at, a_ak_v], axis=1))
        z = z + _dot(p2, z)
        z = z + _dot(_dot(p2, p2), z)
        w_a = z[:, :L]
        u_0 = z[:, L:]

        rhs2 = jnp.concatenate([z, jnp.concatenate([zero_ll, v_h], axis=1)], axis=0)
        top = _dot(a_r, rhs2)
        gm = g_mid[:, sl]
        q_eff = (rt + top[:, :L]) * gm
        m_low = _dot_tn(bh, w_a) * gm
        g_t = _dot_tn(jnp.concatenate([u_0, v_h], axis=0), jnp.concatenate([bh, kh], axis=0))

        ht = ht_ref[hd]
        y_h = _dot_nt(q_eff, ht) + top[:, L:]
        ht_ref[hd] = ht * g_end[:, sl] + _dot_nt(ht, m_low) + g_t

        mean = jnp.mean(y_h, axis=-1, keepdims=True)
        yc = y_h - mean
        var = jnp.mean(yc * yc, axis=-1, keepdims=True)
        y_n = yc * lax.rsqrt(var + TM_LN_EPS) * lng_ref[:, sl] + lnb_ref[:, sl]
        bonus = jnp.sum(rk_prod[:, sl], axis=-1, keepdims=True) * v_h
        outs.append((y_n + bonus) * _silu(g[:, sl]))
    y_ref[:, :W] = jnp.concatenate(outs, axis=1)

    o = TM_COLS
    b_gate = p_ref[:, o:o + W]
    c_gate = p_ref[:, o + W:o + 2 * W]
    hh = p_ref[:, o + 2 * W:o + 3 * W]
    g2 = p_ref[:, o + 3 * W:o + 4 * W]
    u = c_gate * hh
    carry = uc_ref[...]
    conv = (cw_ref[0:1, :] * _shift_rows(u, carry, 2) + cw_ref[1:2, :] * _shift_rows(u, carry, 1) +
            cw_ref[2:3, :] * u)
    uc_ref[...] = u[L - 8:L, :]
    y_ref[:, W:] = b_gate * conv * _silu(g2)


def _even_mixers(p, mu, w0, w2, a0, a2, k_k, k_a, r_k, lnx_g, lnx_b, conv_w):
    bsz, t, cols = p.shape
    row = lambda a: a.reshape(1, -1)
    full = lambda a: pl.BlockSpec(a.shape, lambda b, c: (0,) * a.ndim)
    params = [row(mu), row(w0), w2, row(a0), a2, row(k_k), row(k_a), row(r_k), row(lnx_g), row(lnx_b),
              conv_w]
    return pl.pallas_call(
        _even_mixer_kernel,
        grid=(bsz, t // CHUNK),
        in_specs=[pl.BlockSpec((None, CHUNK, cols), lambda b, c: (b, c, 0))] + [full(a) for a in params],
        out_specs=pl.BlockSpec((None, CHUNK, 2 * WIDTH), lambda b, c: (b, c, 0)),
        out_shape=jax.ShapeDtypeStruct((bsz, t, 2 * WIDTH), F32),
        scratch_shapes=[
            pltpu.VMEM((1, TM_COLS), F32),
            pltpu.VMEM((HEADS, HEAD, HEAD), F32),
            pltpu.VMEM((8, WIDTH), F32),
        ],
        compiler_params=pltpu.CompilerParams(
            dimension_semantics=("parallel", "arbitrary"), vmem_limit_bytes=VMEM_LIMIT),
        name="even_mixers",
    )(p, *params)


def _odd_mixer_kernel(p_ref, cfw_ref, cfb_ref, cfg_ref, cfbb_ref, scw_ref, scb_ref, dtb_ref, alog_ref,
                      dsk_ref, ng_ref, y_ref, ubuf_ref, xc_ref, st_ref):
    L = CHUNK
    W = WIDTH

    @pl.when(pl.program_id(1) == 0)
    def _():
        ubuf_ref[...] = jnp.zeros_like(ubuf_ref)
        xc_ref[...] = jnp.zeros_like(xc_ref)
        st_ref[...] = jnp.zeros_like(st_ref)

    val = p_ref[:, 0:W]
    glu = p_ref[:, W:2 * W]
    g = p_ref[:, 2 * W:3 * W]
    ubuf_ref[CF_HALO:, :] = val * _sigmoid(glu)
    acc = jnp.zeros((L, W), F32) + cfb_ref[...]
    for i in range(CF_KERNEL):
        off = CF_HALO - (CF_KERNEL - 1) + i
        acc = acc + cfw_ref[i:i + 1, :] * ubuf_ref[off:off + L, :]
    ubuf_ref[:CF_HALO, :] = ubuf_ref[L:L + CF_HALO, :]
    mean = jnp.mean(acc, axis=-1, keepdims=True)
    ac = acc - mean
    var = jnp.mean(ac * ac, axis=-1, keepdims=True)
    ln = ac * lax.rsqrt(var + CF_LN_EPS) * cfg_ref[...] + cfbb_ref[...]
    y_ref[:, :W] = _silu(ln) * _silu(g)

    o = 3 * W
    z = p_ref[:, o:o + W]
    xbc_in = p_ref[:, o + W:o + W + SSD_XBC]
    dt_in = p_ref[:, o + W + SSD_XBC:o + W + SSD_XBC + LANES]
    carry = xc_ref[...]
    conv = (scw_ref[0:1, :] * _shift_rows(xbc_in, carry, 3) + scw_ref[1:2, :] * _shift_rows(xbc_in, carry, 2) +
            scw_ref[2:3, :] * _shift_rows(xbc_in, carry, 1) + scw_ref[3:4, :] * xbc_in + scb_ref[...])
    xc_ref[...] = xbc_in[L - 8:L, :]
    xbc = _silu(conv)
    xs = xbc[:, :W]
    dt = _softplus(dt_in + dtb_ref[...])
    da = dt * (-jnp.exp(alog_ref[...]))
    ii = lax.broadcasted_iota(jnp.int32, (L, L), 0)
    jj = lax.broadcasted_iota(jnp.int32, (L, L), 1)
    causal = ii >= jj
    cs = _dot_exact_lhs(causal.astype(F32), da)
    e8 = (lax.broadcasted_iota(jnp.int32, (8, LANES), 0) ==
          lax.broadcasted_iota(jnp.int32, (8, LANES), 1)).astype(BF16)
    cs_hi, cs_lo = _split_hi_lo(cs)
    nt = (((1,), (1,)), ((), ()))
    cs_t = (lax.dot_general(e8, cs_hi, nt, preferred_element_type=F32) +
            lax.dot_general(e8, cs_lo, nt, preferred_element_type=F32))

    outs = []
    for grp in range(SSD_GROUPS):
        bm = xbc[:, W + grp * SSD_STATE:W + (grp + 1) * SSD_STATE]
        cm = xbc[:, W + (SSD_GROUPS + grp) * SSD_STATE:W + (SSD_GROUPS + grp + 1) * SSD_STATE]
        cb = _dot_nt(cm, bm)
        for e in range(SSD_HPG):
            hd = grp * SSD_HPG + e
            sl = slice(hd * HEAD, (hd + 1) * HEAD)
            col = cs[:, hd:hd + 1]
            seg = col - cs_t[hd:hd + 1, :]
            lmat = jnp.exp(jnp.where(causal, seg, -1e30))
            x_dt = xs[:, sl] * dt[:, hd:hd + 1]
            y_diag = _dot(cb * lmat, x_dt)
            last = cs[L - 1:L, hd:hd + 1]
            state = st_ref[hd]
            y_off = _dot(cm, state) * jnp.exp(col)
            st_ref[hd] = state * jnp.exp(last) + _dot_tn(bm, x_dt * jnp.exp(last - col))
            outs.append(y_diag + y_off)
    y = jnp.concatenate(outs, axis=1) + xs * dsk_ref[...]
    y = y * _silu(z)
    ms = jnp.mean(y * y, axis=-1, keepdims=True)
    y_ref[:, W:] = y * lax.rsqrt(ms + NORM_EPS) * ng_ref[...]


def _odd_mixers(p, cf_w, cf_b, cf_g, cf_bb, sc_w, sc_b, dt_bias, a_log, d_skip, norm_g):
    bsz, t, cols = p.shape
    row = lambda a: a.reshape(1, -1)
    pad_lanes = lambda a: jnp.zeros((1, LANES), F32).at[0, :a.shape[0]].set(a)
    full = lambda a: pl.BlockSpec(a.shape, lambda b, c: (0,) * a.ndim)
    params = [cf_w, row(cf_b), row(cf_g), row(cf_bb), sc_w, row(sc_b), pad_lanes(dt_bias), pad_lanes(a_log),
              row(jnp.repeat(d_skip, HEAD)), row(norm_g)]
    return pl.pallas_call(
        _odd_mixer_kernel,
        grid=(bsz, t // CHUNK),
        in_specs=[pl.BlockSpec((None, CHUNK, cols), lambda b, c: (b, c, 0))] + [full(a) for a in params],
        out_specs=pl.BlockSpec((None, CHUNK, 2 * WIDTH), lambda b, c: (b, c, 0)),
        out_shape=jax.ShapeDtypeStruct((bsz, t, 2 * WIDTH), F32),
        scratch_shapes=[
            pltpu.VMEM((CF_HALO + CHUNK, WIDTH), F32),
            pltpu.VMEM((8, SSD_XBC), F32),
            pltpu.VMEM((HEADS, SSD_STATE, HEAD), F32),
        ],
        compiler_params=pltpu.CompilerParams(
            dimension_semantics=("parallel", "arbitrary"), vmem_limit_bytes=VMEM_LIMIT),
        name="odd_mixers",
    )(p, *params)


def kernel(x, c, ada_w, ada_b, norm_pre, norm_post, ev_w_in, ev_w_out, tm_mu, tm_w0, tm_w2, tm_a0, tm_a2,
           tm_k_k, tm_k_a, tm_r_k, tm_lnx_g, tm_lnx_b, sc_conv_w, od_w_in, od_w_out, cf_conv_w, cf_conv_b,
           cf_ln_g, cf_ln_b, ssd_conv_w, ssd_conv_b, ssd_dt_bias, ssd_a_log, ssd_d, ssd_norm_g):
    depth = ada_w.shape[0]
    bsz, t, d = x.shape
    assert d == D_MODEL and t % ROW_TILE == 0 and t % CHUNK == 0
    mod = _ada_modulation(c, ada_w, ada_b)
    od_cols = od_w_in.shape[-1]
    od_w_pad = jnp.zeros(od_w_in.shape[:2] + (ODD_COLS_PAD,), BF16).at[..., :od_cols].set(od_w_in.astype(BF16))
    for i in range(depth):
        shift = mod[i, :, None, 0:d]
        scale1 = 1.0 + mod[i, :, None, d:2 * d]
        gate = mod[i, :, None, 2 * d:3 * d]
        j = i // 2
        if i % 2 == 0:
            p = _in_projection(x, norm_pre[i], scale1, shift, ev_w_in[j].astype(BF16))
            y = _even_mixers(p, tm_mu[j], tm_w0[j], tm_w2[j], tm_a0[j], tm_a2[j], tm_k_k[j].reshape(-1),
                             tm_k_a[j].reshape(-1), tm_r_k[j].reshape(-1), tm_lnx_g[j], tm_lnx_b[j],
                             sc_conv_w[j])
            w_out = ev_w_out[j]
        else:
            p = _in_projection(x, norm_pre[i], scale1, shift, od_w_pad[j])
            y = _odd_mixers(p, cf_conv_w[j], cf_conv_b[j], cf_ln_g[j], cf_ln_b[j], ssd_conv_w[j],
                            ssd_conv_b[j], ssd_dt_bias[j], ssd_a_log[j], ssd_d[j], ssd_norm_g[j])
            w_out = od_w_out[j]
        x = _out_projection(y, x, norm_post[i], gate, w_out.astype(BF16))
    return x
```

```python
import functools

import jax
import jax.numpy as jnp
from jax import lax
from jax.experimental import pallas as pl
from jax.experimental.pallas import tpu as pltpu

F32 = jnp.float32
BF16 = jnp.bfloat16

D_MODEL = 1024
CHUNK = 64
EVEN_TILE = 128
ODD_TILE = 128
HEAD = 64
HEADS = 8
WIDTH = 512
LORA = 64
TM_COLS = 4 * WIDTH + 2 * LORA
EVEN_COLS = TM_COLS + 4 * WIDTH
SSD_STATE = 128
SSD_GROUPS = 2
SSD_HPG = HEADS // SSD_GROUPS
SSD_XBC = WIDTH + 2 * SSD_GROUPS * SSD_STATE
CF_KERNEL = 31
CF_HALO = 32
LANES = 128
ODD_COLS_PAD = 3 * WIDTH + WIDTH + SSD_XBC + LANES
NORM_EPS = 1e-6
TM_LN_EPS = 64e-5
CF_LN_EPS = 1e-5
ROW_TILE = 256
VMEM_LIMIT = 48 * 1024 * 1024


def _dot(a, b):
    return jnp.dot(a.astype(BF16), b.astype(BF16), preferred_element_type=F32)


def _dot_nt(a, b):
    return lax.dot_general(a.astype(BF16), b.astype(BF16), (((1,), (1,)), ((), ())),
                           preferred_element_type=F32)


def _dot_tn(a, b):
    return lax.dot_general(a.astype(BF16), b.astype(BF16), (((0,), (0,)), ((), ())),
                           preferred_element_type=F32)


def _split_hi_lo(x):
    hi = x.astype(BF16)
    lo = (x - hi.astype(F32)).astype(BF16)
    return hi, lo


def _dot_exact_lhs(m, x):
    hi, lo = _split_hi_lo(x)
    mb = m.astype(BF16)
    return (jnp.dot(mb, hi, preferred_element_type=F32) +
            jnp.dot(mb, lo, preferred_element_type=F32))


def _sigmoid(x):
    return 1.0 / (1.0 + jnp.exp(-x))


def _silu(x):
    return x * _sigmoid(x)


def _softplus(x):
    return jnp.maximum(x, 0.0) + jnp.log(1.0 + jnp.exp(-jnp.abs(x)))


def _shift_rows(x, carry, d):
    n = x.shape[0]
    row = lax.broadcasted_iota(jnp.int32, (n, 1), 0)
    out = pltpu.roll(x, d, 0)
    for i in range(d):
        out = jnp.where(row == i, carry[8 - d + i:8 - d + i + 1, :], out)
    return out


def _ada_kernel(c_ref, w_ref, b_ref, o_ref):
    ca = _silu(c_ref[...])
    o_ref[...] = jnp.dot(ca, w_ref[...], preferred_element_type=F32,
                         precision=lax.Precision.HIGHEST) + b_ref[...]


def _ada_modulation(c, ada_w, ada_b):
    depth, d, d3 = ada_w.shape
    bsz = c.shape[0]
    rows = 8
    c_pad = jnp.zeros((rows, d), F32).at[:bsz].set(c)
    tn = 768
    out = pl.pallas_call(
        _ada_kernel,
        grid=(depth, d3 // tn),
        in_specs=[
            pl.BlockSpec((rows, d), lambda i, j: (0, 0)),
            pl.BlockSpec((None, d, tn), lambda i, j: (i, 0, j)),
            pl.BlockSpec((None, 1, tn), lambda i, j: (i, 0, j)),
        ],
        out_specs=pl.BlockSpec((None, rows, tn), lambda i, j: (i, 0, j)),
        out_shape=jax.ShapeDtypeStruct((depth, rows, d3), F32),
        compiler_params=pltpu.CompilerParams(
            dimension_semantics=("parallel", "parallel"), vmem_limit_bytes=VMEM_LIMIT),
        name="ada_modulation",
    )(c_pad, ada_w, ada_b.reshape(depth, 1, d3))
    return out[:, :bsz]


def _inproj_kernel(x_ref, g_ref, sc_ref, sh_ref, w_ref, o_ref):
    x = x_ref[...]
    ms = jnp.mean(x * x, axis=-1, keepdims=True)
    h = x * lax.rsqrt(ms + NORM_EPS) * g_ref[...]
    h = h * sc_ref[...] + sh_ref[...]
    o_ref[...] = _dot(h, w_ref[...])


def _in_projection(x, g, scale1, shift, w):
    bsz, t, d = x.shape
    cols = w.shape[1]
    return pl.pallas_call(
        _inproj_kernel,
        grid=(bsz, t // ROW_TILE),
        in_specs=[
            pl.BlockSpec((None, ROW_TILE, d), lambda b, i: (b, i, 0)),
            pl.BlockSpec((1, d), lambda b, i: (0, 0)),
            pl.BlockSpec((None, 1, d), lambda b, i: (b, 0, 0)),
            pl.BlockSpec((None, 1, d), lambda b, i: (b, 0, 0)),
            pl.BlockSpec((d, cols), lambda b, i: (0, 0)),
        ],
        out_specs=pl.BlockSpec((None, ROW_TILE, cols), lambda b, i: (b, i, 0)),
        out_shape=jax.ShapeDtypeStruct((bsz, t, cols), F32),
        compiler_params=pltpu.CompilerParams(
            dimension_semantics=("parallel", "parallel"), vmem_limit_bytes=VMEM_LIMIT),
        name="in_projection",
    )(x, g.reshape(1, d), scale1, shift, w)


def _outproj_kernel(y_ref, x_ref, g_ref, gate_ref, w_ref, o_ref):
    yo = _dot(y_ref[...], w_ref[...])
    ms = jnp.mean(yo * yo, axis=-1, keepdims=True)
    n = yo * lax.rsqrt(ms + NORM_EPS) * g_ref[...]
    o_ref[...] = x_ref[...] + gate_ref[...] * n


def _out_projection(y, x, g, gate, w):
    bsz, t, d = x.shape
    k = y.shape[-1]
    return pl.pallas_call(
        _outproj_kernel,
        grid=(bsz, t // ROW_TILE),
        in_specs=[
            pl.BlockSpec((None, ROW_TILE, k), lambda b, i: (b, i, 0)),
            pl.BlockSpec((None, ROW_TILE, d), lambda b, i: (b, i, 0)),
            pl.BlockSpec((1, d), lambda b, i: (0, 0)),
            pl.BlockSpec((None, 1, d), lambda b, i: (b, 0, 0)),
            pl.BlockSpec((k, d), lambda b, i: (0, 0)),
        ],
        out_specs=pl.BlockSpec((None, ROW_TILE, d), lambda b, i: (b, i, 0)),
        out_shape=jax.ShapeDtypeStruct((bsz, t, d), F32),
        compiler_params=pltpu.CompilerParams(
            dimension_semantics=("parallel", "parallel"), vmem_limit_bytes=VMEM_LIMIT),
        name="out_projection",
    )(y, x, g.reshape(1, d), gate, w)


def _bd(x):
    lane = lax.broadcasted_iota(jnp.int32, x.shape, 1)
    return jnp.concatenate([jnp.where(lane < HEAD, x, 0.0), jnp.where(lane >= HEAD, x, 0.0)],
                           axis=0).astype(BF16)


def _bd2(z):
    return jnp.concatenate([_bd(z[:, :LANES]), _bd(z[:, LANES:])], axis=1)


def _tn_pair(x, y):
    full = _dot_tn(x, y)
    lane = lax.broadcasted_iota(jnp.int32, (HEAD, LANES), 1)
    return jnp.where(lane < HEAD, full[:HEAD], full[HEAD:])


def _seg_sum(x, ones_bd):
    rows = x.shape[0]
    n = x.shape[1] // LANES
    xs = jnp.concatenate([x[:, j * LANES:(j + 1) * LANES] for j in range(n)], axis=0)
    hi, lo = _split_hi_lo(xs)
    s = (jnp.dot(hi, ones_bd, preferred_element_type=F32) + jnp.dot(lo, ones_bd, preferred_element_type=F32))
    return jnp.concatenate([s[j * rows:(j + 1) * rows] for j in range(n)], axis=1)


def _even_mixer_kernel(p_ref, mu_ref, w0_ref, w2_ref, a0_ref, a2_ref, kk_ref, ka_ref, rk_ref,
                       lng_ref, lnb_ref, cw_ref, y_ref, prow_ref, ht_ref, uc_ref):
    L = CHUNK
    T = p_ref.shape[0]
    nsub = T // L
    W = WIDTH
    cat = jnp.concatenate

    @pl.when(pl.program_id(1) == 0)
    def _():
        prow_ref[...] = jnp.zeros_like(prow_ref)
        ht_ref[...] = jnp.zeros_like(ht_ref)
        uc_ref[...] = jnp.zeros_like(uc_ref)

    p_tm = p_ref[:, :TM_COLS]
    row = lax.broadcasted_iota(jnp.int32, (T, 1), 0)
    prev = jnp.where(row == 0, prow_ref[...], pltpu.roll(p_tm, 1, 0))
    prow_ref[...] = p_tm[T - 1:T, :]
    ps = p_tm + (prev - p_tm) * mu_ref[...]
    r = ps[:, 0:W]
    k = ps[:, W:2 * W]
    v = ps[:, 2 * W:3 * W]
    g = ps[:, 3 * W:4 * W]
    wd = ps[:, 4 * W:4 * W + LORA]
    ad = ps[:, 4 * W + LORA:4 * W + 2 * LORA]
    w_log = -_softplus(-(w0_ref[...] + _dot(jnp.tanh(wd), w2_ref[...]))) - 0.5
    lw = -jnp.exp(w_log)
    a_icl = _sigmoid(a0_ref[...] + _dot(ad, a2_ref[...]))
    k2 = k * (1.0 + (a_icl - 1.0) * ka_ref[...])
    ones_bd = ((lax.broadcasted_iota(jnp.int32, (LANES, LANES), 0) >> 6) ==
               (lax.broadcasted_iota(jnp.int32, (LANES, LANES), 1) >> 6)).astype(BF16)
    kk = k * kk_ref[...]
    kk = kk * lax.rsqrt(jnp.maximum(_seg_sum(kk * kk, ones_bd), 1e-24))
    kb = kk * a_icl

    ti = lax.broadcasted_iota(jnp.int32, (T, T), 0)
    tj = lax.broadcasted_iota(jnp.int32, (T, T), 1)
    tri = ((ti >= tj) & ((ti >> 6) == (tj >> 6))).astype(F32)
    cum_incl = _dot_exact_lhs(tri, lw)
    cum_excl = cum_incl - lw
    rows_of = lambda x, i: x[i:i + 1, :]
    mid_rows = [rows_of(cum_incl, s * L + L // 2 - 1) for s in range(nsub)]
    end_rows = [rows_of(cum_incl, s * L + L - 1) for s in range(nsub)]
    mid = cat([jnp.broadcast_to(m, (L, W)) for m in mid_rows], axis=0)
    end = cat([jnp.broadcast_to(m, (L, W)) for m in end_rows], axis=0)
    g_mid = [jnp.exp(m) for m in mid_rows]
    g_end = [jnp.exp(m) for m in end_rows]
    e_k = jnp.exp(mid - cum_incl)
    e_end = jnp.exp(end - cum_incl)
    at_all = -kk * jnp.exp(cum_excl - mid)
    rt_all = r * jnp.exp(cum_incl - mid)
    bt_all = kb * e_k
    kt_all = k2 * e_k
    bh_all = kb * e_end
    kh_all = k2 * e_end

    ri = lax.broadcasted_iota(jnp.int32, (L, LANES), 0)
    cj = lax.broadcasted_iota(jnp.int32, (L, LANES), 1) & (HEAD - 1)
    eye = (ri == cj).astype(F32)
    strict = ri > cj
    same4 = (ri >> 2) == (cj >> 2)
    same16 = (ri >> 4) == (cj >> 4)
    m0 = strict & same4
    m1 = strict & same16 & jnp.logical_not(same4)
    m2 = strict & jnp.logical_not(same16)
    r2 = lax.broadcasted_iota(jnp.int32, (2 * L, 2 * LANES), 0)
    c2 = lax.broadcasted_iota(jnp.int32, (2 * L, 2 * LANES), 1) & (HEAD - 1)
    mask_aa = (r2 & (L - 1)) >= jnp.where(r2 < L, c2 + 1, c2)
    zero_sq = jnp.zeros((LANES, LANES), BF16)

    cs = [(s, pr) for s in range(nsub) for pr in range(HEADS // 2)]
    tile = lambda x, c: x[c[0] * L:(c[0] + 1) * L, c[1] * LANES:(c[1] + 1) * LANES]
    at = [tile(at_all, c) for c in cs]
    rt = [tile(rt_all, c) for c in cs]
    vv = [tile(v, c) for c in cs]
    bh = [tile(bh_all, c) for c in cs]
    kh = [tile(kh_all, c) for c in cs]
    n = range(len(cs))

    aa = [_dot_nt(cat([at[i], rt[i]], axis=0), cat([_bd(tile(bt_all, c)), _bd(tile(kt_all, c))], axis=0))
          for i, c in enumerate(cs)]
    aa = [jnp.where(mask_aa, x, 0.0) for x in aa]
    n_ab = [x[:L, :LANES] for x in aa]
    akv = [_dot(aa[i][:L, LANES:], _bd(vv[i])) for i in n]

    n0 = [jnp.where(m0, x, 0.0) for x in n_ab]
    n0sq = [_dot(x, _bd(x)) for x in n0]
    d0 = [eye + n0[i] + n0sq[i] + _dot(n0[i], _bd(n0sq[i])) for i in n]
    p1 = [_dot(d0[i], _bd(jnp.where(m1, n_ab[i], 0.0))) for i in n]
    p1sq = [_dot(x, _bd(x)) for x in p1]
    e1 = [d0[i] + _dot(p1[i], _bd(d0[i])) for i in n]
    d1 = [e1[i] + _dot(p1sq[i], _bd(e1[i])) for i in n]
    p2 = [_dot(d1[i], _bd(jnp.where(m2, n_ab[i], 0.0))) for i in n]
    z = [_dot(d1[i], cat([_bd(at[i]), _bd(akv[i])], axis=1)) for i in n]
    p2sq = [_dot(x, _bd(x)) for x in p2]
    z = [z[i] + _dot(p2[i], _bd2(z[i])) for i in n]
    z = [z[i] + _dot(p2sq[i], _bd2(z[i])) for i in n]

    top = [_dot(aa[i][L:, :], cat([_bd2(z[i]), cat([zero_sq, _bd(vv[i])], axis=1)], axis=0))
           for i in n]
    gm = [g_mid[c[0]][:, c[1] * LANES:(c[1] + 1) * LANES] for c in cs]
    ge = [g_end[c[0]][:, c[1] * LANES:(c[1] + 1) * LANES] for c in cs]
    q_eff = [(rt[i] + top[i][:, :LANES]) * gm[i] for i in n]
    m_low = [_tn_pair(bh[i], z[i][:, :LANES]) * gm[i] for i in n]
    g_t = [_tn_pair(cat([z[i][:, LANES:], vv[i]], axis=0), cat([bh[i], kh[i]], axis=0)) for i in n]

    state = [ht_ref[pr] for pr in range(HEADS // 2)]
    y_tiles = []
    for i, (s, pr) in enumerate(cs):
        y_tiles.append(_dot_nt(q_eff[i], _bd(state[pr])) + top[i][:, LANES:])
        state[pr] = state[pr] * ge[i] + _dot_nt(state[pr], _bd(m_low[i])) + g_t[i]
    for pr in range(HEADS // 2):
        ht_ref[pr] = state[pr]

    npr = HEADS // 2
    y = cat([cat(y_tiles[s * npr:(s + 1) * npr], axis=1) for s in range(nsub)], axis=0)
    inv_n = 1.0 / HEAD
    yc = y - _seg_sum(y, ones_bd) * inv_n
    var = _seg_sum(yc * yc, ones_bd) * inv_n
    y_n = yc * lax.rsqrt(var + TM_LN_EPS) * lng_ref[...] + lnb_ref[...]
    bonus = _seg_sum(r * k2 * rk_ref[...], ones_bd) * v
    y_ref[:, :W] = (y_n + bonus) * _silu(g)

    o = TM_COLS
    b_gate = p_ref[:, o:o + W]
    c_gate = p_ref[:, o + W:o + 2 * W]
    hh = p_ref[:, o + 2 * W:o + 3 * W]
    g2 = p_ref[:, o + 3 * W:o + 4 * W]
    u = c_gate * hh
    carry = uc_ref[...]
    conv = (cw_ref[0:1, :] * _shift_rows(u, carry, 2) + cw_ref[1:2, :] * _shift_rows(u, carry, 1) +
            cw_ref[2:3, :] * u)
    uc_ref[...] = u[T - 8:T, :]
    y_ref[:, W:] = b_gate * conv * _silu(g2)


def _even_mixers(p, mu, w0, w2, a0, a2, k_k, k_a, r_k, lnx_g, lnx_b, conv_w):
    bsz, t, cols = p.shape
    row = lambda a: a.reshape(1, -1)
    full = lambda a: pl.BlockSpec(a.shape, lambda b, c: (0,) * a.ndim)
    params = [row(mu), row(w0), w2, row(a0), a2, row(k_k), row(k_a), row(r_k), row(lnx_g), row(lnx_b),
              conv_w]
    return pl.pallas_call(
        _even_mixer_kernel,
        grid=(bsz, t // EVEN_TILE),
        in_specs=[pl.BlockSpec((None, EVEN_TILE, cols), lambda b, c: (b, c, 0))] + [full(a) for a in params],
        out_specs=pl.BlockSpec((None, EVEN_TILE, 2 * WIDTH), lambda b, c: (b, c, 0)),
        out_shape=jax.ShapeDtypeStruct((bsz, t, 2 * WIDTH), F32),
        scratch_shapes=[
            pltpu.VMEM((1, TM_COLS), F32),
            pltpu.VMEM((HEADS // 2, HEAD, LANES), F32),
            pltpu.VMEM((8, WIDTH), F32),
        ],
        compiler_params=pltpu.CompilerParams(
            dimension_semantics=("parallel", "arbitrary"), vmem_limit_bytes=VMEM_LIMIT),
        name="even_mixers",
    )(p, *params)


def _odd_mixer_kernel(p_ref, cfw_ref, cfb_ref, cfg_ref, cfbb_ref, scw_ref, scb_ref, dtb_ref, alog_ref,
                      dsk_ref, ng_ref, y_ref, ubuf_ref, hbuf_ref, xc_ref, st_ref):
    L = CHUNK
    T = p_ref.shape[0]
    nsub = T // L
    W = WIDTH
    cat = jnp.concatenate

    @pl.when(pl.program_id(1) == 0)
    def _():
        ubuf_ref[...] = jnp.zeros_like(ubuf_ref)
        xc_ref[...] = jnp.zeros_like(xc_ref)
        st_ref[...] = jnp.zeros_like(st_ref)

    val = p_ref[:, 0:W]
    glu = p_ref[:, W:2 * W]
    g = p_ref[:, 2 * W:3 * W]
    ubuf_ref[CF_HALO:CF_HALO + T, :] = val * _sigmoid(glu)
    first = CF_HALO - (CF_KERNEL - 1)
    acc = jnp.zeros((T, W), F32) + cfb_ref[...]
    for rr in range(8):
        part = None
        for i in range(CF_KERNEL):
            off = first + i
            if off % 8 != rr:
                continue
            term = cfw_ref[i:i + 1, :] * ubuf_ref[off - rr:off - rr + T + 8, :]
            part = term if part is None else part + term
        if rr == 0:
            acc = acc + part[:T]
        else:
            hbuf_ref[...] = part
            acc = acc + hbuf_ref[rr:rr + T, :]
    ubuf_ref[:CF_HALO, :] = ubuf_ref[T:T + CF_HALO, :]
    mean = jnp.mean(acc, axis=-1, keepdims=True)
    ac = acc - mean
    var = jnp.mean(ac * ac, axis=-1, keepdims=True)
    ln = ac * lax.rsqrt(var + CF_LN_EPS) * cfg_ref[...] + cfbb_ref[...]
    y_ref[:, :W] = _silu(ln) * _silu(g)

    o = 3 * W
    z = p_ref[:, o:o + W]
    xbc_in = p_ref[:, o + W:o + W + SSD_XBC]
    dt_in = p_ref[:, o + W + SSD_XBC:o + W + SSD_XBC + LANES]
    carry = xc_ref[...]
    conv = (scw_ref[0:1, :] * _shift_rows(xbc_in, carry, 3) + scw_ref[1:2, :] * _shift_rows(xbc_in, carry, 2) +
            scw_ref[2:3, :] * _shift_rows(xbc_in, carry, 1) + scw_ref[3:4, :] * xbc_in + scb_ref[...])
    xc_ref[...] = xbc_in[T - 8:T, :]
    xbc = _silu(conv)
    xs = xbc[:, :W]
    dt = _softplus(dt_in + dtb_ref[...])
    da = dt * (-jnp.exp(alog_ref[...]))
    ti = lax.broadcasted_iota(jnp.int32, (T, T), 0)
    tj = lax.broadcasted_iota(jnp.int32, (T, T), 1)
    tri = ((ti >= tj) & ((ti >> 6) == (tj >> 6))).astype(F32)
    cs = _dot_exact_lhs(tri, da)
    expand = ((lax.broadcasted_iota(jnp.int32, (LANES, W), 1) >> 6) ==
              lax.broadcasted_iota(jnp.int32, (LANES, W), 0)).astype(BF16)

    def widen(x):
        hi, lo = _split_hi_lo(x)
        return (jnp.dot(hi, expand, preferred_element_type=F32) + jnp.dot(lo, expand, preferred_element_type=F32))

    cs_w = widen(cs)
    x_dt = xs * widen(dt)
    ri = lax.broadcasted_iota(jnp.int32, (L, W), 0)
    cj = lax.broadcasted_iota(jnp.int32, (L, W), 1) & (HEAD - 1)
    diag = (ri == cj).astype(F32)
    causal = ri >= cj
    ones_ll = jnp.ones((L, L), F32)

    y_chunks = []
    state = [st_ref[pr] for pr in range(HEADS // 2)]
    for s in range(nsub):
        rows = slice(s * L, (s + 1) * L)
        csw = cs_w[rows]
        last = csw[L - 1:L, :]
        row_b = _dot_exact_lhs(ones_ll, csw * diag)
        lmat = jnp.exp(jnp.where(causal, csw - row_b, -1e30))
        xd = x_dt[rows]
        xd_dec = xd * jnp.exp(last - csw)
        e_cs = jnp.exp(csw)
        e_last = jnp.exp(last)
        tiles = []
        for grp in range(SSD_GROUPS):
            bm = xbc[rows, W + grp * SSD_STATE:W + (grp + 1) * SSD_STATE]
            cm = xbc[rows, W + (SSD_GROUPS + grp) * SSD_STATE:W + (SSD_GROUPS + grp + 1) * SSD_STATE]
            cb2 = _dot_nt(cm, cat([bm, bm], axis=0))
            for e in range(SSD_HPG // 2):
                pr = grp * (SSD_HPG // 2) + e
                lanes = slice(pr * LANES, (pr + 1) * LANES)
                y_diag = _dot(cb2 * lmat[:, lanes], _bd(xd[:, lanes]))
                y_off = _dot(cm, state[pr]) * e_cs[:, lanes]
                state[pr] = state[pr] * e_last[:, lanes] + _dot_tn(bm, xd_dec[:, lanes])
                tiles.append(y_diag + y_off)
        y_chunks.append(cat(tiles, axis=1))
    for pr in range(HEADS // 2):
        st_ref[pr] = state[pr]
    y = cat(y_chunks, axis=0) + xs * dsk_ref[...]
    y = y * _silu(z)
    ms = jnp.mean(y * y, axis=-1, keepdims=True)
    y_ref[:, W:] = y * lax.rsqrt(ms + NORM_EPS) * ng_ref[...]


def _odd_mixers(p, cf_w, cf_b, cf_g, cf_bb, sc_w, sc_b, dt_bias, a_log, d_skip, norm_g):
    bsz, t, cols = p.shape
    row = lambda a: a.reshape(1, -1)
    pad_lanes = lambda a: jnp.zeros((1, LANES), F32).at[0, :a.shape[0]].set(a)
    full = lambda a: pl.BlockSpec(a.shape, lambda b, c: (0,) * a.ndim)
    params = [cf_w, row(cf_b), row(cf_g), row(cf_bb), sc_w, row(sc_b), pad_lanes(dt_bias), pad_lanes(a_log),
              row(jnp.repeat(d_skip, HEAD)), row(norm_g)]
    return pl.pallas_call(
        _odd_mixer_kernel,
        grid=(bsz, t // ODD_TILE),
        in_specs=[pl.BlockSpec((None, ODD_TILE, cols), lambda b, c: (b, c, 0))] + [full(a) for a in params],
        out_specs=pl.BlockSpec((None, ODD_TILE, 2 * WIDTH), lambda b, c: (b, c, 0)),
        out_shape=jax.ShapeDtypeStruct((bsz, t, 2 * WIDTH), F32),
        scratch_shapes=[
            pltpu.VMEM((CF_HALO + ODD_TILE + 8, WIDTH), F32),
            pltpu.VMEM((ODD_TILE + 8, WIDTH), F32),
            pltpu.VMEM((8, SSD_XBC), F32),
            pltpu.VMEM((HEADS // 2, SSD_STATE, LANES), F32),
        ],
        compiler_params=pltpu.CompilerParams(
            dimension_semantics=("parallel", "arbitrary"), vmem_limit_bytes=VMEM_LIMIT),
        name="odd_mixers",
    )(p, *params)


def kernel(x, c, ada_w, ada_b, norm_pre, norm_post, ev_w_in, ev_w_out, tm_mu, tm_w0, tm_w2, tm_a0, tm_a2,
           tm_k_k, tm_k_a, tm_r_k, tm_lnx_g, tm_lnx_b, sc_conv_w, od_w_in, od_w_out, cf_conv_w, cf_conv_b,
           cf_ln_g, cf_ln_b, ssd_conv_w, ssd_conv_b, ssd_dt_bias, ssd_a_log, ssd_d, ssd_norm_g):
    depth = ada_w.shape[0]
    bsz, t, d = x.shape
    assert d == D_MODEL and t % ROW_TILE == 0 and t % EVEN_TILE == 0 and t % ODD_TILE == 0
    mod = _ada_modulation(c, ada_w, ada_b)
    od_cols = od_w_in.shape[-1]
    od_w_pad = jnp.zeros(od_w_in.shape[:2] + (ODD_COLS_PAD,), BF16).at[..., :od_cols].set(od_w_in.astype(BF16))
    for i in range(depth):
        shift = mod[i, :, None, 0:d]
        scale1 = 1.0 + mod[i, :, None, d:2 * d]
        gate = mod[i, :, None, 2 * d:3 * d]
        j = i // 2
        if i % 2 == 0:
            p = _in_projection(x, norm_pre[i], scale1, shift, ev_w_in[j].astype(BF16))
            y = _even_mixers(p, tm_mu[j], tm_w0[j], tm_w2[j], tm_a0[j], tm_a2[j], tm_k_k[j].reshape(-1),
                             tm_k_a[j].reshape(-1), tm_r_k[j].reshape(-1), tm_lnx_g[j], tm_lnx_b[j],
                             sc_conv_w[j])
            w_out = ev_w_out[j]
        else:
            p = _in_projection(x, norm_pre[i], scale1, shift, od_w_pad[j])
            y = _odd_mixers(p, cf_conv_w[j], cf_conv_b[j], cf_ln_g[j], cf_ln_b[j], ssd_conv_w[j],
                            ssd_conv_b[j], ssd_dt_bias[j], ssd_a_log[j], ssd_d[j], ssd_norm_g[j])
            w_out = od_w_out[j]
        x = _out_projection(y, x, norm_post[i], gate, w_out.astype(BF16))
    return x
```

```python
import functools

import jax
import jax.numpy as jnp
from jax import lax
from jax.experimental import pallas as pl
from jax.experimental.pallas import tpu as pltpu

F32 = jnp.float32
BF16 = jnp.bfloat16

D_MODEL = 1024
CHUNK = 64
EVEN_TILE = 128
ODD_TILE = 128
HEAD = 64
HEADS = 8
WIDTH = 512
LORA = 64
TM_COLS = 4 * WIDTH + 2 * LORA
EVEN_COLS = TM_COLS + 4 * WIDTH
SSD_STATE = 128
SSD_GROUPS = 2
SSD_HPG = HEADS // SSD_GROUPS
SSD_XBC = WIDTH + 2 * SSD_GROUPS * SSD_STATE
CF_KERNEL = 31
CF_HALO = 32
LANES = 128
ODD_COLS_PAD = 3 * WIDTH + WIDTH + SSD_XBC + LANES
NORM_EPS = 1e-6
TM_LN_EPS = 64e-5
CF_LN_EPS = 1e-5
VMEM_LIMIT = 48 * 1024 * 1024


def _dot(a, b):
    return jnp.dot(a.astype(BF16), b.astype(BF16), preferred_element_type=F32)


def _dot_nt(a, b):
    return lax.dot_general(a.astype(BF16), b.astype(BF16), (((1,), (1,)), ((), ())),
                           preferred_element_type=F32)


def _dot_tn(a, b):
    return lax.dot_general(a.astype(BF16), b.astype(BF16), (((0,), (0,)), ((), ())),
                           preferred_element_type=F32)


def _split_hi_lo(x):
    hi = x.astype(BF16)
    lo = (x - hi.astype(F32)).astype(BF16)
    return hi, lo


def _dot_exact_lhs(m, x):
    hi, lo = _split_hi_lo(x)
    mb = m.astype(BF16)
    return (jnp.dot(mb, hi, preferred_element_type=F32) +
            jnp.dot(mb, lo, preferred_element_type=F32))


def _sigmoid(x):
    return 1.0 / (1.0 + jnp.exp(-x))


def _silu(x):
    return x * _sigmoid(x)


def _softplus(x):
    return jnp.maximum(x, 0.0) + jnp.log(1.0 + jnp.exp(-jnp.abs(x)))


def _shift_rows(x, carry, d):
    n = x.shape[0]
    row = lax.broadcasted_iota(jnp.int32, (n, 1), 0)
    out = pltpu.roll(x, d, 0)
    for i in range(d):
        out = jnp.where(row == i, carry[8 - d + i:8 - d + i + 1, :], out)
    return out


def _ada_kernel(c_ref, w_ref, b_ref, o_ref):
    ca = _silu(c_ref[...])
    o_ref[...] = jnp.dot(ca, w_ref[...], preferred_element_type=F32,
                         precision=lax.Precision.HIGHEST) + b_ref[...]


def _ada_modulation(c, ada_w, ada_b):
    depth, d, d3 = ada_w.shape
    bsz = c.shape[0]
    rows = 8
    c_pad = jnp.zeros((rows, d), F32).at[:bsz].set(c)
    tn = 768
    out = pl.pallas_call(
        _ada_kernel,
        grid=(depth, d3 // tn),
        in_specs=[
            pl.BlockSpec((rows, d), lambda i, j: (0, 0)),
            pl.BlockSpec((None, d, tn), lambda i, j: (i, 0, j)),
            pl.BlockSpec((None, 1, tn), lambda i, j: (i, 0, j)),
        ],
        out_specs=pl.BlockSpec((None, rows, tn), lambda i, j: (i, 0, j)),
        out_shape=jax.ShapeDtypeStruct((depth, rows, d3), F32),
        compiler_params=pltpu.CompilerParams(
            dimension_semantics=("parallel", "parallel"), vmem_limit_bytes=VMEM_LIMIT),
        name="ada_modulation",
    )(c_pad, ada_w, ada_b.reshape(depth, 1, d3))
    return out[:, :bsz]


PIPE_LAG = 2


PROJ_BLOCK = 3 * LANES


def _projection_emitter(xn_ref, npre_ref, sc_ref, sh_ref, win_ref, p_ref):
    x = xn_ref[...]
    ms = jnp.mean(x * x, axis=-1, keepdims=True)
    h = x * lax.rsqrt(ms + NORM_EPS) * npre_ref[...]
    h = (h * sc_ref[...] + sh_ref[...]).astype(BF16)
    cols = win_ref.shape[1]
    starts = list(range(0, cols, PROJ_BLOCK))

    def emit(k=None):
        for _ in range(len(starts) if k is None else min(k, len(starts))):
            lo = starts.pop(0)
            hi = min(lo + PROJ_BLOCK, cols)
            p_ref[:, lo:hi] = jnp.dot(h, win_ref[:, lo:hi], preferred_element_type=F32)

    return emit


def _residual_update(y_ref, x_ref, wout_ref, npost_ref, gate_ref, o_ref):
    yo = jnp.dot(y_ref[...], wout_ref[...], preferred_element_type=F32)
    ms = jnp.mean(yo * yo, axis=-1, keepdims=True)
    o_ref[...] = x_ref[...] + gate_ref[...] * (yo * lax.rsqrt(ms + NORM_EPS) * npost_ref[...])


def _layer_call(body, name, tile, x, scale1, shift, gate, norm_pre, w_in, norm_post, w_out, params, scratch):
    bsz, t, d = x.shape
    n_tiles = t // tile
    cols = w_in.shape[1]
    row = lambda a: a.reshape(1, -1)
    full = lambda a: pl.BlockSpec(a.shape, lambda b, c: (0,) * a.ndim)
    per_batch = pl.BlockSpec((None, 1, d), lambda b, c: (b, 0, 0))
    xn_spec = pl.BlockSpec((None, tile, d), lambda b, c: (b, jnp.minimum(c, n_tiles - 1), 0))
    xr_spec = pl.BlockSpec((None, tile, d), lambda b, c: (b, jnp.maximum(c - PIPE_LAG, 0), 0))
    shared = [row(norm_pre), w_in, row(norm_post), w_out] + list(params)
    return pl.pallas_call(
        body,
        grid=(bsz, n_tiles + PIPE_LAG),
        in_specs=[xn_spec, xr_spec, per_batch, per_batch, per_batch] + [full(a) for a in shared],
        out_specs=xr_spec,
        out_shape=jax.ShapeDtypeStruct((bsz, t, d), F32),
        scratch_shapes=[
            pltpu.VMEM((tile, cols), F32),
            pltpu.VMEM((tile, 2 * WIDTH), BF16),
        ] + list(scratch),
        compiler_params=pltpu.CompilerParams(
            dimension_semantics=("parallel", "arbitrary"), vmem_limit_bytes=VMEM_LIMIT),
        name=name,
    )(x, x, scale1, shift, gate, *shared)


def _bd(x):
    lane = lax.broadcasted_iota(jnp.int32, x.shape, 1)
    return jnp.concatenate([jnp.where(lane < HEAD, x, 0.0), jnp.where(lane >= HEAD, x, 0.0)],
                           axis=0).astype(BF16)


def _bd2(z):
    return jnp.concatenate([_bd(z[:, :LANES]), _bd(z[:, LANES:])], axis=1)


def _tn_pair(x, y):
    full = _dot_tn(x, y)
    lane = lax.broadcasted_iota(jnp.int32, (HEAD, LANES), 1)
    return jnp.where(lane < HEAD, full[:HEAD], full[HEAD:])


def _seg_sum(x, ones_bd):
    rows = x.shape[0]
    n = x.shape[1] // LANES
    xs = jnp.concatenate([x[:, j * LANES:(j + 1) * LANES] for j in range(n)], axis=0)
    s = jnp.dot(xs.astype(BF16), ones_bd, preferred_element_type=F32)
    return jnp.concatenate([s[j * rows:(j + 1) * rows] for j in range(n)], axis=1)


def _even_layer_kernel(xn_ref, xr_ref, sc_ref, sh_ref, gate_ref, npre_ref, win_ref, npost_ref, wout_ref,
                       mu_ref, w0_ref, w2_ref, a0_ref, a2_ref, kk_ref, ka_ref, rk_ref, lng_ref, lnb_ref, cw_ref,
                       o_ref, pbuf_ref, y_ref, prow_ref, ht_ref, uc_ref):
    L = CHUNK
    T = xn_ref.shape[0]
    nsub = T // L
    W = WIDTH
    cat = jnp.concatenate
    step = pl.program_id(1)

    @pl.when(step == 0)
    def _():
        pbuf_ref[...] = jnp.zeros_like(pbuf_ref)
        y_ref[...] = jnp.zeros_like(y_ref)

    @pl.when(step <= 1)
    def _():
        prow_ref[...] = jnp.zeros_like(prow_ref)
        ht_ref[...] = jnp.zeros_like(ht_ref)
        uc_ref[...] = jnp.zeros_like(uc_ref)

    _residual_update(y_ref, xr_ref, wout_ref, npost_ref, gate_ref, o_ref)

    p_ref = pbuf_ref
    p_tm = p_ref[:, :TM_COLS]
    o = TM_COLS
    b_gate = p_ref[:, o:o + W]
    c_gate = p_ref[:, o + W:o + 2 * W]
    hh = p_ref[:, o + 2 * W:o + 3 * W]
    g2 = p_ref[:, o + 3 * W:o + 4 * W]
    emit_proj = _projection_emitter(xn_ref, npre_ref, sc_ref, sh_ref, win_ref, pbuf_ref)

    u = c_gate * hh
    carry = uc_ref[...]
    conv = (cw_ref[0:1, :] * _shift_rows(u, carry, 2) + cw_ref[1:2, :] * _shift_rows(u, carry, 1) +
            cw_ref[2:3, :] * u)
    uc_ref[...] = u[T - 8:T, :]
    y_ref[:, W:] = (b_gate * conv * _silu(g2)).astype(BF16)
    emit_proj(1)

    row = lax.broadcasted_iota(jnp.int32, (T, 1), 0)
    prev = jnp.where(row == 0, prow_ref[...], pltpu.roll(p_tm, 1, 0))
    prow_ref[...] = p_tm[T - 1:T, :]
    ps = p_tm + (prev - p_tm) * mu_ref[...]
    emit_proj(2)
    r = ps[:, 0:W]
    k = ps[:, W:2 * W]
    v = ps[:, 2 * W:3 * W]
    g = ps[:, 3 * W:4 * W]
    wd = ps[:, 4 * W:4 * W + LORA]
    ad = ps[:, 4 * W + LORA:4 * W + 2 * LORA]
    w_log = -_softplus(-(w0_ref[...] + _dot(jnp.tanh(wd), w2_ref[...]))) - 0.5
    lw = -jnp.exp(w_log)
    emit_proj(1)
    a_icl = _sigmoid(a0_ref[...] + _dot(ad, a2_ref[...]))
    k2 = k * (1.0 + (a_icl - 1.0) * ka_ref[...])
    ones_bd = ((lax.broadcasted_iota(jnp.int32, (LANES, LANES), 0) >> 6) ==
               (lax.broadcasted_iota(jnp.int32, (LANES, LANES), 1) >> 6)).astype(BF16)
    kk = k * kk_ref[...]
    kk = kk * lax.rsqrt(jnp.maximum(_seg_sum(kk * kk, ones_bd), 1e-24))
    kb = kk * a_icl
    emit_proj(1)

    ti = lax.broadcasted_iota(jnp.int32, (T, T), 0)
    tj = lax.broadcasted_iota(jnp.int32, (T, T), 1)
    tri = ((ti >= tj) & ((ti >> 6) == (tj >> 6))).astype(F32)
    cum_incl = _dot_exact_lhs(tri, lw)
    cum_excl = cum_incl - lw
    rows_of = lambda x, i: x[i:i + 1, :]
    mid_rows = [rows_of(cum_incl, s * L + L // 2 - 1) for s in range(nsub)]
    end_rows = [rows_of(cum_incl, s * L + L - 1) for s in range(nsub)]
    mid = cat([jnp.broadcast_to(m, (L, W)) for m in mid_rows], axis=0)
    end = cat([jnp.broadcast_to(m, (L, W)) for m in end_rows], axis=0)
    g_mid = [jnp.exp(m) for m in mid_rows]
    g_end = [jnp.exp(m) for m in end_rows]
    e_k = jnp.exp(mid - cum_incl)
    e_end = jnp.exp(end - cum_incl)
    emit_proj(1)
    at_all = -kk * jnp.exp(cum_excl - mid)
    rt_all = r * jnp.exp(cum_incl - mid)
    bt_all = kb * e_k
    kt_all = k2 * e_k
    bh_all = kb * e_end
    kh_all = k2 * e_end
    emit_proj(1)

    ri = lax.broadcasted_iota(jnp.int32, (L, LANES), 0)
    cj = lax.broadcasted_iota(jnp.int32, (L, LANES), 1) & (HEAD - 1)
    eye = (ri == cj).astype(F32)
    strict = ri > cj
    same4 = (ri >> 2) == (cj >> 2)
    same16 = (ri >> 4) == (cj >> 4)
    m0 = strict & same4
    m1 = strict & same16 & jnp.logical_not(same4)
    m2 = strict & jnp.logical_not(same16)
    r2 = lax.broadcasted_iota(jnp.int32, (2 * L, 2 * LANES), 0)
    c2 = lax.broadcasted_iota(jnp.int32, (2 * L, 2 * LANES), 1) & (HEAD - 1)
    mask_aa = (r2 & (L - 1)) >= jnp.where(r2 < L, c2 + 1, c2)
    zero_sq = jnp.zeros((LANES, LANES), BF16)

    cs = [(s, pr) for s in range(nsub) for pr in range(HEADS // 2)]
    tile = lambda x, c: x[c[0] * L:(c[0] + 1) * L, c[1] * LANES:(c[1] + 1) * LANES]
    at = [tile(at_all, c) for c in cs]
    rt = [tile(rt_all, c) for c in cs]
    vv = [tile(v, c) for c in cs]
    bh = [tile(bh_all, c) for c in cs]
    kh = [tile(kh_all, c) for c in cs]
    n = range(len(cs))

    aa = [_dot_nt(cat([at[i], rt[i]], axis=0), cat([_bd(tile(bt_all, c)), _bd(tile(kt_all, c))], axis=0))
          for i, c in enumerate(cs)]
    aa = [jnp.where(mask_aa, x, 0.0) for x in aa]
    n_ab = [x[:L, :LANES] for x in aa]
    akv = [_dot(aa[i][:L, LANES:], _bd(vv[i])) for i in n]

    n0 = [jnp.where(m0, x, 0.0) for x in n_ab]
    n0sq = [_dot(x, _bd(x)) for x in n0]
    d0 = [eye + n0[i] + n0sq[i] + _dot(n0[i], _bd(n0sq[i])) for i in n]
    p1 = [_dot(d0[i], _bd(jnp.where(m1, n_ab[i], 0.0))) for i in n]
    p1sq = [_dot(x, _bd(x)) for x in p1]
    e1 = [d0[i] + _dot(p1[i], _bd(d0[i])) for i in n]
    d1 = [e1[i] + _dot(p1sq[i], _bd(e1[i])) for i in n]
    p2 = [_dot(d1[i], _bd(jnp.where(m2, n_ab[i], 0.0))) for i in n]
    z = [_dot(d1[i], cat([_bd(at[i]), _bd(akv[i])], axis=1)) for i in n]
    p2sq = [_dot(x, _bd(x)) for x in p2]
    z = [z[i] + _dot(p2[i], _bd2(z[i])) for i in n]
    z = [z[i] + _dot(p2sq[i], _bd2(z[i])) for i in n]

    top = [_dot(aa[i][L:, :], cat([_bd2(z[i]), cat([zero_sq, _bd(vv[i])], axis=1)], axis=0))
           for i in n]
    gm = [g_mid[c[0]][:, c[1] * LANES:(c[1] + 1) * LANES] for c in cs]
    ge = [g_end[c[0]][:, c[1] * LANES:(c[1] + 1) * LANES] for c in cs]
    q_eff = [(rt[i] + top[i][:, :LANES]) * gm[i] for i in n]
    m_low = [_tn_pair(bh[i], z[i][:, :LANES]) * gm[i] for i in n]
    g_t = [_tn_pair(cat([z[i][:, LANES:], vv[i]], axis=0), cat([bh[i], kh[i]], axis=0)) for i in n]

    state = [ht_ref[pr] for pr in range(HEADS // 2)]
    y_tiles = []
    for i, (s, pr) in enumerate(cs):
        y_tiles.append(_dot_nt(q_eff[i], _bd(state[pr])) + top[i][:, LANES:])
        state[pr] = state[pr] * ge[i] + _dot_nt(state[pr], _bd(m_low[i])) + g_t[i]
    for pr in range(HEADS // 2):
        ht_ref[pr] = state[pr]

    npr = HEADS // 2
    y = cat([cat(y_tiles[s * npr:(s + 1) * npr], axis=1) for s in range(nsub)], axis=0)
    inv_n = 1.0 / HEAD
    yc = y - _seg_sum(y, ones_bd) * inv_n
    emit_proj(1)
    var = _seg_sum(yc * yc, ones_bd) * inv_n
    y_n = yc * lax.rsqrt(var + TM_LN_EPS) * lng_ref[...] + lnb_ref[...]
    emit_proj(1)
    bonus = _seg_sum(r * k2 * rk_ref[...], ones_bd) * v
    y_ref[:, :W] = ((y_n + bonus) * _silu(g)).astype(BF16)
    emit_proj()


def _even_layer(x, scale1, shift, gate, norm_pre, w_in, norm_post, w_out, mu, w0, w2, a0, a2, k_k, k_a, r_k,
                lnx_g, lnx_b, conv_w):
    row = lambda a: a.reshape(1, -1)
    params = [row(mu), row(w0), w2, row(a0), a2, row(k_k), row(k_a), row(r_k), row(lnx_g), row(lnx_b), conv_w]
    scratch = [
        pltpu.VMEM((1, TM_COLS), F32),
        pltpu.VMEM((HEADS // 2, HEAD, LANES), F32),
        pltpu.VMEM((8, WIDTH), F32),
    ]
    return _layer_call(_even_layer_kernel, "even_layer", EVEN_TILE, x, scale1, shift, gate, norm_pre, w_in,
                       norm_post, w_out, params, scratch)


def _odd_layer_kernel(xn_ref, xr_ref, sc_ref, sh_ref, gate_ref, npre_ref, win_ref, npost_ref, wout_ref,
                      cfw_ref, cfb_ref, cfg_ref, cfbb_ref, scw_ref, scb_ref, dtb_ref, alog_ref, dsk_ref, ng_ref,
                      o_ref, pbuf_ref, y_ref, ubuf_ref, hbuf_ref, xc_ref, st_ref):
    L = CHUNK
    T = xn_ref.shape[0]
    nsub = T // L
    W = WIDTH
    cat = jnp.concatenate
    step = pl.program_id(1)

    @pl.when(step == 0)
    def _():
        pbuf_ref[...] = jnp.zeros_like(pbuf_ref)
        y_ref[...] = jnp.zeros_like(y_ref)

    @pl.when(step <= 1)
    def _():
        ubuf_ref[...] = jnp.zeros_like(ubuf_ref)
        xc_ref[...] = jnp.zeros_like(xc_ref)
        st_ref[...] = jnp.zeros_like(st_ref)

    _residual_update(y_ref, xr_ref, wout_ref, npost_ref, gate_ref, o_ref)

    p_ref = pbuf_ref
    val = p_ref[:, 0:W]
    glu = p_ref[:, W:2 * W]
    g = p_ref[:, 2 * W:3 * W]
    o = 3 * W
    z = p_ref[:, o:o + W]
    xbc_in = p_ref[:, o + W:o + W + SSD_XBC]
    dt_in = p_ref[:, o + W + SSD_XBC:o + W + SSD_XBC + LANES]
    emit_proj = _projection_emitter(xn_ref, npre_ref, sc_ref, sh_ref, win_ref, pbuf_ref)

    ubuf_ref[CF_HALO:CF_HALO + T, :] = val * _sigmoid(glu)
    first = CF_HALO - (CF_KERNEL - 1)
    acc = jnp.zeros((T, W), F32) + cfb_ref[...]
    for rr in range(8):
        part = None
        for i in range(CF_KERNEL):
            off = first + i
            if off % 8 != rr:
                continue
            term = cfw_ref[i:i + 1, :] * ubuf_ref[off - rr:off - rr + T + 8, :]
            part = term if part is None else part + term
        if rr == 0:
            acc = acc + part[:T]
        else:
            hbuf_ref[...] = part
            acc = acc + hbuf_ref[rr:rr + T, :]
        if rr % 2 == 1:
            emit_proj(1)
    ubuf_ref[:CF_HALO, :] = ubuf_ref[T:T + CF_HALO, :]
    mean = jnp.mean(acc, axis=-1, keepdims=True)
    ac = acc - mean
    var = jnp.mean(ac * ac, axis=-1, keepdims=True)
    ln = ac * lax.rsqrt(var + CF_LN_EPS) * cfg_ref[...] + cfbb_ref[...]
    y_ref[:, :W] = (_silu(ln) * _silu(g)).astype(BF16)
    emit_proj(1)

    carry = xc_ref[...]
    conv = (scw_ref[0:1, :] * _shift_rows(xbc_in, carry, 3) + scw_ref[1:2, :] * _shift_rows(xbc_in, carry, 2) +
            scw_ref[2:3, :] * _shift_rows(xbc_in, carry, 1) + scw_ref[3:4, :] * xbc_in + scb_ref[...])
    xc_ref[...] = xbc_in[T - 8:T, :]
    emit_proj(1)
    xbc = _silu(conv)
    emit_proj(1)
    xs = xbc[:, :W]
    dt = _softplus(dt_in + dtb_ref[...])
    da = dt * (-jnp.exp(alog_ref[...]))
    ti = lax.broadcasted_iota(jnp.int32, (T, T), 0)
    tj = lax.broadcasted_iota(jnp.int32, (T, T), 1)
    tri = ((ti >= tj) & ((ti >> 6) == (tj >> 6))).astype(F32)
    cs = _dot_exact_lhs(tri, da)
    expand = ((lax.broadcasted_iota(jnp.int32, (LANES, W), 1) >> 6) ==
              lax.broadcasted_iota(jnp.int32, (LANES, W), 0)).astype(BF16)

    def widen(x):
        hi, lo = _split_hi_lo(x)
        return (jnp.dot(hi, expand, preferred_element_type=F32) + jnp.dot(lo, expand, preferred_element_type=F32))

    cs_w = widen(cs)
    x_dt = xs * widen(dt)
    ri = lax.broadcasted_iota(jnp.int32, (L, W), 0)
    cj = lax.broadcasted_iota(jnp.int32, (L, W), 1) & (HEAD - 1)
    diag = (ri == cj).astype(F32)
    causal = ri >= cj
    ones_ll = jnp.ones((L, L), F32)

    y_chunks = []
    state = [st_ref[pr] for pr in range(HEADS // 2)]
    for s in range(nsub):
        rows = slice(s * L, (s + 1) * L)
        csw = cs_w[rows]
        last = csw[L - 1:L, :]
        row_b = _dot_exact_lhs(ones_ll, csw * diag)
        lmat = jnp.exp(jnp.where(causal, csw - row_b, -1e30))
        xd = x_dt[rows]
        xd_dec = xd * jnp.exp(last - csw)
        e_cs = jnp.exp(csw)
        e_last = jnp.exp(last)
        tiles = []
        for grp in range(SSD_GROUPS):
            bm = xbc[rows, W + grp * SSD_STATE:W + (grp + 1) * SSD_STATE]
            cm = xbc[rows, W + (SSD_GROUPS + grp) * SSD_STATE:W + (SSD_GROUPS + grp + 1) * SSD_STATE]
            cb2 = _dot_nt(cm, cat([bm, bm], axis=0))
            for e in range(SSD_HPG // 2):
                pr = grp * (SSD_HPG // 2) + e
                lanes = slice(pr * LANES, (pr + 1) * LANES)
                y_diag = _dot(cb2 * lmat[:, lanes], _bd(xd[:, lanes]))
                y_off = _dot(cm, state[pr]) * e_cs[:, lanes]
                state[pr] = state[pr] * e_last[:, lanes] + _dot_tn(bm, xd_dec[:, lanes])
                tiles.append(y_diag + y_off)
        y_chunks.append(cat(tiles, axis=1))
    for pr in range(HEADS // 2):
        st_ref[pr] = state[pr]
    y = cat(y_chunks, axis=0) + xs * dsk_ref[...]
    y = y * _silu(z)
    ms = jnp.mean(y * y, axis=-1, keepdims=True)
    y_ref[:, W:] = (y * lax.rsqrt(ms + NORM_EPS) * ng_ref[...]).astype(BF16)
    emit_proj()


def _odd_layer(x, scale1, shift, gate, norm_pre, w_in, norm_post, w_out, cf_w, cf_b, cf_g, cf_bb, sc_w, sc_b,
               dt_bias, a_log, d_skip, norm_g):
    row = lambda a: a.reshape(1, -1)
    pad_lanes = lambda a: jnp.zeros((1, LANES), F32).at[0, :a.shape[0]].set(a)
    params = [cf_w, row(cf_b), row(cf_g), row(cf_bb), sc_w, row(sc_b), pad_lanes(dt_bias), pad_lanes(a_log),
              row(jnp.repeat(d_skip, HEAD)), row(norm_g)]
    scratch = [
        pltpu.VMEM((CF_HALO + ODD_TILE + 8, WIDTH), F32),
        pltpu.VMEM((ODD_TILE + 8, WIDTH), F32),
        pltpu.VMEM((8, SSD_XBC), F32),
        pltpu.VMEM((HEADS // 2, SSD_STATE, LANES), F32),
    ]
    return _layer_call(_odd_layer_kernel, "odd_layer", ODD_TILE, x, scale1, shift, gate, norm_pre, w_in,
                       norm_post, w_out, params, scratch)


def kernel(x, c, ada_w, ada_b, norm_pre, norm_post, ev_w_in, ev_w_out, tm_mu, tm_w0, tm_w2, tm_a0, tm_a2,
           tm_k_k, tm_k_a, tm_r_k, tm_lnx_g, tm_lnx_b, sc_conv_w, od_w_in, od_w_out, cf_conv_w, cf_conv_b,
           cf_ln_g, cf_ln_b, ssd_conv_w, ssd_conv_b, ssd_dt_bias, ssd_a_log, ssd_d, ssd_norm_g):
    depth = ada_w.shape[0]
    bsz, t, d = x.shape
    assert d == D_MODEL and t % EVEN_TILE == 0 and t % ODD_TILE == 0
    mod = _ada_modulation(c, ada_w, ada_b)
    od_cols = od_w_in.shape[-1]
    od_w_pad = jnp.zeros(od_w_in.shape[:2] + (ODD_COLS_PAD,), BF16).at[..., :od_cols].set(od_w_in.astype(BF16))
    for i in range(depth):
        shift = mod[i, :, None, 0:d]
        scale1 = 1.0 + mod[i, :, None, d:2 * d]
        gate = mod[i, :, None, 2 * d:3 * d]
        j = i // 2
        if i % 2 == 0:
            x = _even_layer(x, scale1, shift, gate, norm_pre[i], ev_w_in[j].astype(BF16), norm_post[i],
                            ev_w_out[j].astype(BF16), tm_mu[j], tm_w0[j], tm_w2[j], tm_a0[j], tm_a2[j],
                            tm_k_k[j].reshape(-1), tm_k_a[j].reshape(-1), tm_r_k[j].reshape(-1), tm_lnx_g[j],
                            tm_lnx_b[j], sc_conv_w[j])
        else:
            x = _odd_layer(x, scale1, shift, gate, norm_pre[i], od_w_pad[j], norm_post[i],
                           od_w_out[j].astype(BF16), cf_conv_w[j], cf_conv_b[j], cf_ln_g[j], cf_ln_b[j],
                           ssd_conv_w[j], ssd_conv_b[j], ssd_dt_bias[j], ssd_a_log[j], ssd_d[j], ssd_norm_g[j])
    return x
```

```python
import functools

import jax
import jax.numpy as jnp
from jax import lax
from jax.experimental import pallas as pl
from jax.experimental.pallas import tpu as pltpu

F32 = jnp.float32
BF16 = jnp.bfloat16

D_MODEL = 1024
CHUNK = 64
EVEN_TILE = 256
ODD_TILE = 256
HEAD = 64
HEADS = 8
WIDTH = 512
LORA = 64
TM_COLS = 4 * WIDTH + 2 * LORA
EVEN_COLS = TM_COLS + 4 * WIDTH
SSD_STATE = 128
SSD_GROUPS = 2
SSD_HPG = HEADS // SSD_GROUPS
SSD_XBC = WIDTH + 2 * SSD_GROUPS * SSD_STATE
CF_KERNEL = 31
CF_HALO = 32
LANES = 128
ODD_COLS_PAD = 3 * WIDTH + WIDTH + SSD_XBC + LANES
NORM_EPS = 1e-6
TM_LN_EPS = 64e-5
CF_LN_EPS = 1e-5
VMEM_LIMIT = 48 * 1024 * 1024


def _dot(a, b):
    return jnp.dot(a.astype(BF16), b.astype(BF16), preferred_element_type=F32)


def _dot_nt(a, b):
    return lax.dot_general(a.astype(BF16), b.astype(BF16), (((1,), (1,)), ((), ())),
                           preferred_element_type=F32)


def _dot_tn(a, b):
    return lax.dot_general(a.astype(BF16), b.astype(BF16), (((0,), (0,)), ((), ())),
                           preferred_element_type=F32)


def _split_hi_lo(x):
    hi = x.astype(BF16)
    lo = (x - hi.astype(F32)).astype(BF16)
    return hi, lo


def _dot_exact_lhs(m, x):
    hi, lo = _split_hi_lo(x)
    mb = m.astype(BF16)
    return (jnp.dot(mb, hi, preferred_element_type=F32) +
            jnp.dot(mb, lo, preferred_element_type=F32))


def _sigmoid(x):
    return 0.5 * jnp.tanh(0.5 * x) + 0.5


def _silu(x):
    return x * _sigmoid(x)


def _softplus(x):
    return jnp.maximum(x, 0.0) + jnp.log(1.0 + jnp.exp(-jnp.abs(x)))


def _shift_rows(x, carry, d):
    out = pltpu.roll(x, d, 0)
    row = lax.broadcasted_iota(jnp.int32, (8, 1), 0)
    head = jnp.where(row < d, pltpu.roll(carry, d, 0), out[:8])
    return jnp.concatenate([head, out[8:]], axis=0)


def _ada_kernel(c_ref, w_ref, b_ref, o_ref):
    ca = _silu(c_ref[...])
    o_ref[...] = jnp.dot(ca, w_ref[...], preferred_element_type=F32,
                         precision=lax.Precision.HIGHEST) + b_ref[...]


def _ada_modulation(c, ada_w, ada_b):
    depth, d, d3 = ada_w.shape
    bsz = c.shape[0]
    rows = 8
    c_pad = jnp.zeros((rows, d), F32).at[:bsz].set(c)
    tn = 768
    out = pl.pallas_call(
        _ada_kernel,
        grid=(depth, d3 // tn),
        in_specs=[
            pl.BlockSpec((rows, d), lambda i, j: (0, 0)),
            pl.BlockSpec((None, d, tn), lambda i, j: (i, 0, j)),
            pl.BlockSpec((None, 1, tn), lambda i, j: (i, 0, j)),
        ],
        out_specs=pl.BlockSpec((None, rows, tn), lambda i, j: (i, 0, j)),
        out_shape=jax.ShapeDtypeStruct((depth, rows, d3), F32),
        compiler_params=pltpu.CompilerParams(
            dimension_semantics=("parallel", "parallel"), vmem_limit_bytes=VMEM_LIMIT),
        name="ada_modulation",
    )(c_pad, ada_w, ada_b.reshape(depth, 1, d3))
    return out[:, :bsz]


PIPE_LAG = 2


PROJ_BLOCK = 3 * LANES


def _projection_emitter(xn_ref, npre_ref, sc_ref, sh_ref, win_ref, p_ref):
    x = xn_ref[...]
    ms = jnp.mean(x * x, axis=-1, keepdims=True)
    h = x * lax.rsqrt(ms + NORM_EPS) * npre_ref[...]
    h = (h * sc_ref[...] + sh_ref[...]).astype(BF16)
    cols = win_ref.shape[1]
    starts = list(range(0, cols, PROJ_BLOCK))

    def emit(k=None):
        blk = None
        for _ in range(len(starts) if k is None else min(k, len(starts))):
            lo = starts.pop(0)
            hi = min(lo + PROJ_BLOCK, cols)
            blk = jnp.dot(h, win_ref[:, lo:hi], preferred_element_type=F32)
            p_ref[:, lo:hi] = blk
        return blk

    return emit


def _zero_after(dep, width):
    bits = lax.bitcast_convert_type(dep[0:8, 0:LANES], jnp.int32)
    zero = lax.shift_right_logical(lax.shift_right_logical(bits, 16), 16).astype(F32)
    return jnp.concatenate([zero[0:1, :]] * (width // LANES), axis=1)


def _residual_update(y_ref, x_ref, wout_ref, npost_ref, gate_ref, o_ref):
    yo = jnp.dot(y_ref[...], wout_ref[...], preferred_element_type=F32)
    ms = jnp.mean(yo * yo, axis=-1, keepdims=True)
    o_ref[...] = x_ref[...] + gate_ref[...] * (yo * lax.rsqrt(ms + NORM_EPS) * npost_ref[...])


def _layer_call(body, name, tile, x, scale1, shift, gate, norm_pre, w_in, norm_post, w_out, params, scratch):
    bsz, t, d = x.shape
    n_tiles = t // tile
    cols = w_in.shape[1]
    row = lambda a: a.reshape(1, -1)
    full = lambda a: pl.BlockSpec(a.shape, lambda b, c: (0,) * a.ndim)
    per_batch = pl.BlockSpec((None, 1, d), lambda b, c: (b, 0, 0))
    xn_spec = pl.BlockSpec((None, tile, d), lambda b, c: (b, jnp.minimum(c, n_tiles - 1), 0))
    xr_spec = pl.BlockSpec((None, tile, d), lambda b, c: (b, jnp.maximum(c - PIPE_LAG, 0), 0))
    shared = [row(norm_pre), w_in, row(norm_post), w_out] + list(params)
    return pl.pallas_call(
        body,
        grid=(bsz, n_tiles + PIPE_LAG),
        in_specs=[xn_spec, xr_spec, per_batch, per_batch, per_batch] + [full(a) for a in shared],
        out_specs=xr_spec,
        out_shape=jax.ShapeDtypeStruct((bsz, t, d), F32),
        scratch_shapes=[
            pltpu.VMEM((tile, cols), F32),
            pltpu.VMEM((tile, 2 * WIDTH), BF16),
        ] + list(scratch),
        compiler_params=pltpu.CompilerParams(
            dimension_semantics=("parallel", "arbitrary"), vmem_limit_bytes=VMEM_LIMIT),
        name=name,
    )(x, x, scale1, shift, gate, *shared)


def _bd(x):
    lane = lax.broadcasted_iota(jnp.int32, x.shape, 1)
    return jnp.concatenate([jnp.where(lane < HEAD, x, 0.0), jnp.where(lane >= HEAD, x, 0.0)],
                           axis=0).astype(BF16)


def _bd2(z):
    return jnp.concatenate([_bd(z[:, :LANES]), _bd(z[:, LANES:])], axis=1)


def _tn_pair(x, y):
    full = _dot_tn(x, y)
    lane = lax.broadcasted_iota(jnp.int32, (HEAD, LANES), 1)
    return jnp.where(lane < HEAD, full[:HEAD], full[HEAD:])


def _seg_sum(x, ones_bd):
    rows = x.shape[0]
    n = x.shape[1] // LANES
    xs = jnp.concatenate([x[:, j * LANES:(j + 1) * LANES] for j in range(n)], axis=0)
    s = jnp.dot(xs.astype(BF16), ones_bd, preferred_element_type=F32)
    return jnp.concatenate([s[j * rows:(j + 1) * rows] for j in range(n)], axis=1)


def _even_layer_kernel(xn_ref, xr_ref, sc_ref, sh_ref, gate_ref, npre_ref, win_ref, npost_ref, wout_ref,
                       mu_ref, w0_ref, w2_ref, a0_ref, a2_ref, kk_ref, ka_ref, rk_ref, lng_ref, lnb_ref, cw_ref,
                       o_ref, pbuf_ref, y_ref, prow_ref, ht_ref, uc_ref):
    L = CHUNK
    T = xn_ref.shape[0]
    nsub = T // L
    W = WIDTH
    cat = jnp.concatenate
    step = pl.program_id(1)

    @pl.when(step == 0)
    def _():
        pbuf_ref[...] = jnp.zeros_like(pbuf_ref)
        y_ref[...] = jnp.zeros_like(y_ref)

    @pl.when(step <= 1)
    def _():
        prow_ref[...] = jnp.zeros_like(prow_ref)
        ht_ref[...] = jnp.zeros_like(ht_ref)
        uc_ref[...] = jnp.zeros_like(uc_ref)

    _residual_update(y_ref, xr_ref, wout_ref, npost_ref, gate_ref, o_ref)

    p_ref = pbuf_ref
    p_tm = p_ref[:, :TM_COLS]
    o = TM_COLS
    b_gate = p_ref[:, o:o + W]
    c_gate = p_ref[:, o + W:o + 2 * W]
    hh = p_ref[:, o + 2 * W:o + 3 * W]
    g2 = p_ref[:, o + 3 * W:o + 4 * W]
    emit_proj = _projection_emitter(xn_ref, npre_ref, sc_ref, sh_ref, win_ref, pbuf_ref)

    u = c_gate * hh
    carry = uc_ref[...]
    conv = (cw_ref[0:1, :] * _shift_rows(u, carry, 2) + cw_ref[1:2, :] * _shift_rows(u, carry, 1) +
            cw_ref[2:3, :] * u)
    uc_ref[...] = u[T - 8:T, :]
    y_ref[:, W:] = (b_gate * conv * _silu(g2)).astype(BF16)
    emit_proj(1)

    prev = _shift_rows(p_tm, prow_ref[...], 1)
    prow_ref[...] = p_tm[T - 8:T, :]
    ps = p_tm + (prev - p_tm) * mu_ref[...]
    emit_proj(2)
    r = ps[:, 0:W]
    k = ps[:, W:2 * W]
    v = ps[:, 2 * W:3 * W]
    g = ps[:, 3 * W:4 * W]
    wd = ps[:, 4 * W:4 * W + LORA]
    ad = ps[:, 4 * W + LORA:4 * W + 2 * LORA]
    w_log = -_softplus(-(w0_ref[...] + _dot(jnp.tanh(wd), w2_ref[...]))) - 0.5
    lw = -jnp.exp(w_log)
    emit_proj(1)
    a_icl = _sigmoid(a0_ref[...] + _dot(ad, a2_ref[...]))
    k2 = k * (1.0 + (a_icl - 1.0) * ka_ref[...])
    ones_bd = ((lax.broadcasted_iota(jnp.int32, (LANES, LANES), 0) >> 6) ==
               (lax.broadcasted_iota(jnp.int32, (LANES, LANES), 1) >> 6)).astype(BF16)
    kk = k * kk_ref[...]
    kk = kk * lax.rsqrt(jnp.maximum(_seg_sum(kk * kk, ones_bd), 1e-24))
    kb = kk * a_icl
    emit_proj(1)

    ti = lax.broadcasted_iota(jnp.int32, (T, T), 0)
    tj = lax.broadcasted_iota(jnp.int32, (T, T), 1)
    tri = ((ti >= tj) & ((ti >> 6) == (tj >> 6))).astype(F32)
    cum_incl = _dot_exact_lhs(tri, lw)
    cum_excl = cum_incl - lw
    rows_of = lambda x, i: x[i:i + 1, :]
    mid_rows = [rows_of(cum_incl, s * L + L // 2 - 1) for s in range(nsub)]
    end_rows = [rows_of(cum_incl, s * L + L - 1) for s in range(nsub)]
    mid = cat([jnp.broadcast_to(m, (L, W)) for m in mid_rows], axis=0)
    end = cat([jnp.broadcast_to(m, (L, W)) for m in end_rows], axis=0)
    g_mid = [jnp.exp(m) for m in mid_rows]
    g_end = [jnp.exp(m) for m in end_rows]
    e_k = jnp.exp(mid - cum_incl)
    e_end = jnp.exp(end - cum_incl)
    emit_proj(1)
    at_all = -kk * jnp.exp(cum_excl - mid)
    rt_all = r * jnp.exp(cum_incl - mid)
    bt_all = kb * e_k
    kt_all = k2 * e_k
    bh_all = kb * e_end
    kh_all = k2 * e_end
    emit_proj(1)

    ri = lax.broadcasted_iota(jnp.int32, (L, LANES), 0)
    cj = lax.broadcasted_iota(jnp.int32, (L, LANES), 1) & (HEAD - 1)
    eye = (ri == cj).astype(F32)
    strict = ri > cj
    same4 = (ri >> 2) == (cj >> 2)
    same16 = (ri >> 4) == (cj >> 4)
    m0 = strict & same4
    m1 = strict & same16 & jnp.logical_not(same4)
    m2 = strict & jnp.logical_not(same16)
    r2 = lax.broadcasted_iota(jnp.int32, (2 * L, 2 * LANES), 0)
    c2 = lax.broadcasted_iota(jnp.int32, (2 * L, 2 * LANES), 1) & (HEAD - 1)
    mask_aa = (r2 & (L - 1)) >= jnp.where(r2 < L, c2 + 1, c2)
    zero_sq = jnp.zeros((LANES, LANES), BF16)

    cs = [(s, pr) for s in range(nsub) for pr in range(HEADS // 2)]
    tile = lambda x, c: x[c[0] * L:(c[0] + 1) * L, c[1] * LANES:(c[1] + 1) * LANES]
    at = [tile(at_all, c) for c in cs]
    rt = [tile(rt_all, c) for c in cs]
    vv = [tile(v, c) for c in cs]
    bh = [tile(bh_all, c) for c in cs]
    kh = [tile(kh_all, c) for c in cs]
    n = range(len(cs))

    aa = [_dot_nt(cat([at[i], rt[i]], axis=0), cat([_bd(tile(bt_all, c)), _bd(tile(kt_all, c))], axis=0))
          for i, c in enumerate(cs)]
    aa = [jnp.where(mask_aa, x, 0.0) for x in aa]
    n_ab = [x[:L, :LANES] for x in aa]
    akv = [_dot(aa[i][:L, LANES:], _bd(vv[i])) for i in n]

    n0 = [jnp.where(m0, x, 0.0) for x in n_ab]
    n0sq = [_dot(x, _bd(x)) for x in n0]
    d0 = [eye + n0[i] + n0sq[i] + _dot(n0[i], _bd(n0sq[i])) for i in n]
    p1 = [_dot(d0[i], _bd(jnp.where(m1, n_ab[i], 0.0))) for i in n]
    p1sq = [_dot(x, _bd(x)) for x in p1]
    e1 = [d0[i] + _dot(p1[i], _bd(d0[i])) for i in n]
    d1 = [e1[i] + _dot(p1sq[i], _bd(e1[i])) for i in n]
    p2 = [_dot(d1[i], _bd(jnp.where(m2, n_ab[i], 0.0))) for i in n]
    z = [_dot(d1[i], cat([_bd(at[i]), _bd(akv[i])], axis=1)) for i in n]
    p2sq = [_dot(x, _bd(x)) for x in p2]
    z = [z[i] + _dot(p2[i], _bd2(z[i])) for i in n]
    z = [z[i] + _dot(p2sq[i], _bd2(z[i])) for i in n]

    top = [_dot(aa[i][L:, :], cat([_bd2(z[i]), cat([zero_sq, _bd(vv[i])], axis=1)], axis=0))
           for i in n]
    gm = [g_mid[c[0]][:, c[1] * LANES:(c[1] + 1) * LANES] for c in cs]
    ge = [g_end[c[0]][:, c[1] * LANES:(c[1] + 1) * LANES] for c in cs]
    q_eff = [(rt[i] + top[i][:, :LANES]) * gm[i] for i in n]
    m_low = [_tn_pair(bh[i], z[i][:, :LANES]) * gm[i] for i in n]
    g_t = [_tn_pair(cat([z[i][:, LANES:], vv[i]], axis=0), cat([bh[i], kh[i]], axis=0)) for i in n]

    state = [ht_ref[pr] for pr in range(HEADS // 2)]
    y_tiles = []
    for i, (s, pr) in enumerate(cs):
        y_tiles.append(_dot_nt(q_eff[i], _bd(state[pr])) + top[i][:, LANES:])
        state[pr] = state[pr] * ge[i] + _dot_nt(state[pr], _bd(m_low[i])) + g_t[i]
    for pr in range(HEADS // 2):
        ht_ref[pr] = state[pr]

    npr = HEADS // 2
    y = cat([cat(y_tiles[s * npr:(s + 1) * npr], axis=1) for s in range(nsub)], axis=0)
    inv_n = 1.0 / HEAD
    yc = y - _seg_sum(y, ones_bd) * inv_n
    emit_proj(1)
    var = _seg_sum(yc * yc, ones_bd) * inv_n
    y_n = yc * lax.rsqrt(var + TM_LN_EPS) * lng_ref[...] + lnb_ref[...]
    emit_proj(1)
    bonus = _seg_sum(r * k2 * rk_ref[...], ones_bd) * v
    y_ref[:, :W] = ((y_n + bonus) * _silu(g)).astype(BF16)
    emit_proj()


def _even_layer(x, scale1, shift, gate, norm_pre, w_in, norm_post, w_out, mu, w0, w2, a0, a2, k_k, k_a, r_k,
                lnx_g, lnx_b, conv_w):
    row = lambda a: a.reshape(1, -1)
    params = [row(mu), row(w0), w2, row(a0), a2, row(k_k), row(k_a), row(r_k), row(lnx_g), row(lnx_b), conv_w]
    scratch = [
        pltpu.VMEM((8, TM_COLS), F32),
        pltpu.VMEM((HEADS // 2, HEAD, LANES), F32),
        pltpu.VMEM((8, WIDTH), F32),
    ]
    return _layer_call(_even_layer_kernel, "even_layer", EVEN_TILE, x, scale1, shift, gate, norm_pre, w_in,
                       norm_post, w_out, params, scratch)


def _odd_layer_kernel(xn_ref, xr_ref, sc_ref, sh_ref, gate_ref, npre_ref, win_ref, npost_ref, wout_ref,
                      cfw_ref, cfb_ref, cfg_ref, cfbb_ref, scw_ref, scb_ref, dtb_ref, alog_ref, dsk_ref, ng_ref,
                      o_ref, pbuf_ref, y_ref, ubuf_ref, xc_ref, st_ref):
    L = CHUNK
    T = xn_ref.shape[0]
    nsub = T // L
    W = WIDTH
    cat = jnp.concatenate
    step = pl.program_id(1)

    @pl.when(step == 0)
    def _():
        pbuf_ref[...] = jnp.zeros_like(pbuf_ref)
        y_ref[...] = jnp.zeros_like(y_ref)

    @pl.when(step <= 1)
    def _():
        ubuf_ref[...] = jnp.zeros_like(ubuf_ref)
        xc_ref[...] = jnp.zeros_like(xc_ref)
        st_ref[...] = jnp.zeros_like(st_ref)

    _residual_update(y_ref, xr_ref, wout_ref, npost_ref, gate_ref, o_ref)

    p_ref = pbuf_ref
    val = p_ref[:, 0:W]
    glu = p_ref[:, W:2 * W]
    g = p_ref[:, 2 * W:3 * W]
    o = 3 * W
    z = p_ref[:, o:o + W]
    xbc_in = p_ref[:, o + W:o + W + SSD_XBC]
    dt_in = p_ref[:, o + W + SSD_XBC:o + W + SSD_XBC + LANES]
    emit_proj = _projection_emitter(xn_ref, npre_ref, sc_ref, sh_ref, win_ref, pbuf_ref)

    ubuf_ref[CF_HALO:CF_HALO + T, :] = val * _sigmoid(glu)
    first = CF_HALO - (CF_KERNEL - 1)
    acc = jnp.zeros((T, W), F32) + cfb_ref[...]
    proj_blk = None
    for rr in range(8):
        part = None
        for i in range(CF_KERNEL):
            off = first + i
            if off % 8 != rr:
                continue
            w_row = cfw_ref[i:i + 1, :]
            if part is None and proj_blk is not None:
                w_row = w_row + _zero_after(proj_blk, W)
            term = w_row * ubuf_ref[off - rr:off - rr + T + 8, :]
            part = term if part is None else part + term
        if rr == 0:
            acc = acc + part[:T]
        else:
            acc = acc + pltpu.roll(part, T + 8 - rr, 0)[:T]
        proj_blk = emit_proj(1)
    ubuf_ref[:CF_HALO, :] = ubuf_ref[T:T + CF_HALO, :]
    mean = jnp.mean(acc, axis=-1, keepdims=True)
    ac = acc - mean
    var = jnp.mean(ac * ac, axis=-1, keepdims=True)
    ln = ac * lax.rsqrt(var + CF_LN_EPS) * cfg_ref[...] + cfbb_ref[...]
    y_ref[:, :W] = (_silu(ln) * _silu(g)).astype(BF16)
    emit_proj(1)

    carry = xc_ref[...]
    conv = (scw_ref[0:1, :] * _shift_rows(xbc_in, carry, 3) + scw_ref[1:2, :] * _shift_rows(xbc_in, carry, 2) +
            scw_ref[2:3, :] * _shift_rows(xbc_in, carry, 1) + scw_ref[3:4, :] * xbc_in + scb_ref[...])
    xc_ref[...] = xbc_in[T - 8:T, :]
    emit_proj(1)
    xbc = _silu(conv)
    emit_proj(1)
    xs = xbc[:, :W]
    dt = _softplus(dt_in + dtb_ref[...])
    da = dt * (-jnp.exp(alog_ref[...]))
    ti = lax.broadcasted_iota(jnp.int32, (T, T), 0)
    tj = lax.broadcasted_iota(jnp.int32, (T, T), 1)
    tri = ((ti >= tj) & ((ti >> 6) == (tj >> 6))).astype(F32)
    cs = _dot_exact_lhs(tri, da)
    expand = ((lax.broadcasted_iota(jnp.int32, (LANES, W), 1) >> 6) ==
              lax.broadcasted_iota(jnp.int32, (LANES, W), 0)).astype(BF16)

    def widen(x):
        hi, lo = _split_hi_lo(x)
        return (jnp.dot(hi, expand, preferred_element_type=F32) + jnp.dot(lo, expand, preferred_element_type=F32))

    cs_w = widen(cs)
    x_dt = xs * widen(dt)
    ri = lax.broadcasted_iota(jnp.int32, (L, W), 0)
    cj = lax.broadcasted_iota(jnp.int32, (L, W), 1) & (HEAD - 1)
    diag = (ri == cj).astype(F32)
    causal = ri >= cj
    ones_ll = jnp.ones((L, L), F32)

    y_chunks = []
    state = [st_ref[pr] for pr in range(HEADS // 2)]
    for s in range(nsub):
        rows = slice(s * L, (s + 1) * L)
        csw = cs_w[rows]
        last = csw[L - 1:L, :]
        row_b = _dot_exact_lhs(ones_ll, csw * diag)
        lmat = jnp.exp(jnp.where(causal, csw - row_b, -1e30))
        xd = x_dt[rows]
        xd_dec = xd * jnp.exp(last - csw)
        e_cs = jnp.exp(csw)
        e_last = jnp.exp(last)
        tiles = []
        for grp in range(SSD_GROUPS):
            bm = xbc[rows, W + grp * SSD_STATE:W + (grp + 1) * SSD_STATE]
            cm = xbc[rows, W + (SSD_GROUPS + grp) * SSD_STATE:W + (SSD_GROUPS + grp + 1) * SSD_STATE]
            cb2 = _dot_nt(cm, cat([bm, bm], axis=0))
            for e in range(SSD_HPG // 2):
                pr = grp * (SSD_HPG // 2) + e
                lanes = slice(pr * LANES, (pr + 1) * LANES)
                y_diag = _dot(cb2 * lmat[:, lanes], _bd(xd[:, lanes]))
                y_off = _dot(cm, state[pr]) * e_cs[:, lanes]
                state[pr] = state[pr] * e_last[:, lanes] + _dot_tn(bm, xd_dec[:, lanes])
                tiles.append(y_diag + y_off)
        y_chunks.append(cat(tiles, axis=1))
    for pr in range(HEADS // 2):
        st_ref[pr] = state[pr]
    y = cat(y_chunks, axis=0) + xs * dsk_ref[...]
    y = y * _silu(z)
    ms = jnp.mean(y * y, axis=-1, keepdims=True)
    y_ref[:, W:] = (y * lax.rsqrt(ms + NORM_EPS) * ng_ref[...]).astype(BF16)
    emit_proj()


def _odd_layer(x, scale1, shift, gate, norm_pre, w_in, norm_post, w_out, cf_w, cf_b, cf_g, cf_bb, sc_w, sc_b,
               dt_bias, a_log, d_skip, norm_g):
    row = lambda a: a.reshape(1, -1)
    pad_lanes = lambda a: jnp.zeros((1, LANES), F32).at[0, :a.shape[0]].set(a)
    params = [cf_w, row(cf_b), row(cf_g), row(cf_bb), sc_w, row(sc_b), pad_lanes(dt_bias), pad_lanes(a_log),
              row(jnp.repeat(d_skip, HEAD)), row(norm_g)]
    scratch = [
        pltpu.VMEM((CF_HALO + ODD_TILE + 8, WIDTH), F32),
        pltpu.VMEM((8, SSD_XBC), F32),
        pltpu.VMEM((HEADS // 2, SSD_STATE, LANES), F32),
    ]
    return _layer_call(_odd_layer_kernel, "odd_layer", ODD_TILE, x, scale1, shift, gate, norm_pre, w_in,
                       norm_post, w_out, params, scratch)


def kernel(x, c, ada_w, ada_b, norm_pre, norm_post, ev_w_in, ev_w_out, tm_mu, tm_w0, tm_w2, tm_a0, tm_a2,
           tm_k_k, tm_k_a, tm_r_k, tm_lnx_g, tm_lnx_b, sc_conv_w, od_w_in, od_w_out, cf_conv_w, cf_conv_b,
           cf_ln_g, cf_ln_b, ssd_conv_w, ssd_conv_b, ssd_dt_bias, ssd_a_log, ssd_d, ssd_norm_g):
    depth = ada_w.shape[0]
    bsz, t, d = x.shape
    assert d == D_MODEL and t % EVEN_TILE == 0 and t % ODD_TILE == 0
    mod = _ada_modulation(c, ada_w, ada_b)
    od_cols = od_w_in.shape[-1]
    od_w_pad = jnp.zeros(od_w_in.shape[:2] + (ODD_COLS_PAD,), BF16).at[..., :od_cols].set(od_w_in.astype(BF16))
    for i in range(depth):
        shift = mod[i, :, None, 0:d]
        scale1 = 1.0 + mod[i, :, None, d:2 * d]
        gate = mod[i, :, None, 2 * d:3 * d]
        j = i // 2
        if i % 2 == 0:
            x = _even_layer(x, scale1, shift, gate, norm_pre[i], ev_w_in[j].astype(BF16), norm_post[i],
                            ev_w_out[j].astype(BF16), tm_mu[j], tm_w0[j], tm_w2[j], tm_a0[j], tm_a2[j],
                            tm_k_k[j].reshape(-1), tm_k_a[j].reshape(-1), tm_r_k[j].reshape(-1), tm_lnx_g[j],
                            tm_lnx_b[j], sc_conv_w[j])
        else:
            x = _odd_layer(x, scale1, shift, gate, norm_pre[i], od_w_pad[j], norm_post[i],
                           od_w_out[j].astype(BF16), cf_conv_w[j], cf_conv_b[j], cf_ln_g[j], cf_ln_b[j],
                           ssd_conv_w[j], ssd_conv_b[j], ssd_dt_bias[j], ssd_a_log[j], ssd_d[j], ssd_norm_g[j])
    return x
```

```python
import functools

import jax
import jax.numpy as jnp
from jax import lax
from jax.experimental import pallas as pl
from jax.experimental.pallas import tpu as pltpu

F32 = jnp.float32
BF16 = jnp.bfloat16

D_MODEL = 1024
CHUNK = 64
EVEN_TILE = 256
ODD_TILE = 256
HEAD = 64
HEADS = 8
WIDTH = 512
LORA = 64
TM_COLS = 4 * WIDTH + 2 * LORA
EVEN_COLS = TM_COLS + 4 * WIDTH
SSD_STATE = 128
SSD_GROUPS = 2
SSD_HPG = HEADS // SSD_GROUPS
SSD_XBC = WIDTH + 2 * SSD_GROUPS * SSD_STATE
CF_KERNEL = 31
CF_HALO = 32
LANES = 128
ODD_COLS_PAD = 3 * WIDTH + WIDTH + SSD_XBC + LANES
NORM_EPS = 1e-6
TM_LN_EPS = 64e-5
CF_LN_EPS = 1e-5
VMEM_LIMIT = 48 * 1024 * 1024


def _dot(a, b):
    return jnp.dot(a.astype(BF16), b.astype(BF16), preferred_element_type=F32)


def _dot_nt(a, b):
    return lax.dot_general(a.astype(BF16), b.astype(BF16), (((1,), (1,)), ((), ())),
                           preferred_element_type=F32)


def _dot_tn(a, b):
    return lax.dot_general(a.astype(BF16), b.astype(BF16), (((0,), (0,)), ((), ())),
                           preferred_element_type=F32)


def _split_hi_lo(x):
    hi = x.astype(BF16)
    lo = (x - hi.astype(F32)).astype(BF16)
    return hi, lo


def _dot_exact_lhs(m, x):
    hi, lo = _split_hi_lo(x)
    mb = m.astype(BF16)
    return (jnp.dot(mb, hi, preferred_element_type=F32) +
            jnp.dot(mb, lo, preferred_element_type=F32))


def _sigmoid(x):
    return 0.5 * jnp.tanh(0.5 * x) + 0.5


def _silu(x):
    return x * _sigmoid(x)


def _softplus(x):
    return jnp.maximum(x, 0.0) + jnp.log(1.0 + jnp.exp(-jnp.abs(x)))


def _shift_rows(x, carry, d):
    out = pltpu.roll(x, d, 0)
    row = lax.broadcasted_iota(jnp.int32, (8, 1), 0)
    head = jnp.where(row < d, pltpu.roll(carry, d, 0), out[:8])
    return jnp.concatenate([head, out[8:]], axis=0)


def _ada_kernel(c_ref, w_ref, b_ref, o_ref):
    ca = _silu(c_ref[...])
    o_ref[...] = jnp.dot(ca, w_ref[...], preferred_element_type=F32,
                         precision=lax.Precision.HIGHEST) + b_ref[...]


def _ada_modulation(c, ada_w, ada_b):
    depth, d, d3 = ada_w.shape
    bsz = c.shape[0]
    rows = 8
    c_pad = jnp.zeros((rows, d), F32).at[:bsz].set(c)
    tn = 768
    out = pl.pallas_call(
        _ada_kernel,
        grid=(depth, d3 // tn),
        in_specs=[
            pl.BlockSpec((rows, d), lambda i, j: (0, 0)),
            pl.BlockSpec((None, d, tn), lambda i, j: (i, 0, j)),
            pl.BlockSpec((None, 1, tn), lambda i, j: (i, 0, j)),
        ],
        out_specs=pl.BlockSpec((None, rows, tn), lambda i, j: (i, 0, j)),
        out_shape=jax.ShapeDtypeStruct((depth, rows, d3), F32),
        compiler_params=pltpu.CompilerParams(
            dimension_semantics=("parallel", "parallel"), vmem_limit_bytes=VMEM_LIMIT),
        name="ada_modulation",
    )(c_pad, ada_w, ada_b.reshape(depth, 1, d3))
    return out[:, :bsz]


PIPE_LAG = 2


PROJ_BLOCK = 3 * LANES


def _projection_emitter(xn_ref, npre_ref, sc_ref, sh_ref, win_ref, p_ref):
    x = xn_ref[...]
    ms = jnp.mean(x * x, axis=-1, keepdims=True)
    h = x * lax.rsqrt(ms + NORM_EPS) * npre_ref[...]
    h = (h * sc_ref[...] + sh_ref[...]).astype(BF16)
    cols = win_ref.shape[1]
    starts = list(range(0, cols, PROJ_BLOCK))

    def emit(k=None):
        blk = None
        for _ in range(len(starts) if k is None else min(k, len(starts))):
            lo = starts.pop(0)
            hi = min(lo + PROJ_BLOCK, cols)
            blk = jnp.dot(h, win_ref[:, lo:hi], preferred_element_type=F32)
            p_ref[:, lo:hi] = blk
        return blk

    return emit


def _zero_after(dep, width):
    bits = lax.bitcast_convert_type(dep[0:8, 0:LANES], jnp.int32)
    zero = lax.shift_right_logical(lax.shift_right_logical(bits, 16), 16).astype(F32)
    return jnp.concatenate([zero[0:1, :]] * (width // LANES), axis=1)


def _residual_update(y_ref, x_ref, wout_ref, npost_ref, gate_ref, o_ref):
    yo = jnp.dot(y_ref[...], wout_ref[...], preferred_element_type=F32)
    ms = jnp.mean(yo * yo, axis=-1, keepdims=True)
    o_ref[...] = x_ref[...] + gate_ref[...] * (yo * lax.rsqrt(ms + NORM_EPS) * npost_ref[...])


def _layer_call(body, name, tile, x, scale1, shift, gate, norm_pre, w_in, norm_post, w_out, params, scratch):
    bsz, t, d = x.shape
    n_tiles = t // tile
    last = bsz * n_tiles - 1
    cols = w_in.shape[1]
    row = lambda a: a.reshape(1, -1)
    full = lambda a: pl.BlockSpec(a.shape, lambda s: (0,) * a.ndim)
    tile_in = lambda s: jnp.minimum(s, last)
    tile_out = lambda s: jnp.maximum(s - PIPE_LAG, 0)
    xn_spec = pl.BlockSpec((tile, d), lambda s: (tile_in(s), 0))
    xr_spec = pl.BlockSpec((tile, d), lambda s: (tile_out(s), 0))
    mod_in = pl.BlockSpec((None, 1, d), lambda s: (tile_in(s) // n_tiles, 0, 0))
    mod_out = pl.BlockSpec((None, 1, d), lambda s: (tile_out(s) // n_tiles, 0, 0))
    shared = [row(norm_pre), w_in, row(norm_post), w_out] + list(params)
    x2 = x.reshape(bsz * t, d)
    out = pl.pallas_call(
        functools.partial(body, n_tiles=n_tiles),
        grid=(bsz * n_tiles + PIPE_LAG,),
        in_specs=[xn_spec, xr_spec, mod_in, mod_in, mod_out] + [full(a) for a in shared],
        out_specs=xr_spec,
        out_shape=jax.ShapeDtypeStruct((bsz * t, d), F32),
        scratch_shapes=[
            pltpu.VMEM((tile, cols), F32),
            pltpu.VMEM((tile, 2 * WIDTH), BF16),
        ] + list(scratch),
        compiler_params=pltpu.CompilerParams(
            dimension_semantics=("arbitrary",), vmem_limit_bytes=VMEM_LIMIT),
        name=name,
    )(x2, x2, scale1, shift, gate, *shared)
    return out.reshape(bsz, t, d)


def _bd(x):
    lane = lax.broadcasted_iota(jnp.int32, x.shape, 1)
    return jnp.concatenate([jnp.where(lane < HEAD, x, 0.0), jnp.where(lane >= HEAD, x, 0.0)],
                           axis=0).astype(BF16)


def _bd2(z):
    return jnp.concatenate([_bd(z[:, :LANES]), _bd(z[:, LANES:])], axis=1)


def _tn_pair(x, y):
    full = _dot_tn(x, y)
    lane = lax.broadcasted_iota(jnp.int32, (HEAD, LANES), 1)
    return jnp.where(lane < HEAD, full[:HEAD], full[HEAD:])


def _seg_sum(x, ones_bd):
    rows = x.shape[0]
    n = x.shape[1] // LANES
    xs = jnp.concatenate([x[:, j * LANES:(j + 1) * LANES] for j in range(n)], axis=0)
    s = jnp.dot(xs.astype(BF16), ones_bd, preferred_element_type=F32)
    return jnp.concatenate([s[j * rows:(j + 1) * rows] for j in range(n)], axis=1)


def _even_layer_kernel(xn_ref, xr_ref, sc_ref, sh_ref, gate_ref, npre_ref, win_ref, npost_ref, wout_ref,
                       mu_ref, w0_ref, w2_ref, a0_ref, a2_ref, kk_ref, ka_ref, rk_ref, lng_ref, lnb_ref, cw_ref,
                       o_ref, pbuf_ref, y_ref, prow_ref, ht_ref, uc_ref, *, n_tiles):
    L = CHUNK
    T = xn_ref.shape[0]
    nsub = T // L
    W = WIDTH
    cat = jnp.concatenate
    step = pl.program_id(0)

    @pl.when(step == 0)
    def _():
        pbuf_ref[...] = jnp.zeros_like(pbuf_ref)
        y_ref[...] = jnp.zeros_like(y_ref)

    @pl.when((step == 0) | ((step + n_tiles - 1) % n_tiles == 0))
    def _():
        prow_ref[...] = jnp.zeros_like(prow_ref)
        ht_ref[...] = jnp.zeros_like(ht_ref)
        uc_ref[...] = jnp.zeros_like(uc_ref)

    _residual_update(y_ref, xr_ref, wout_ref, npost_ref, gate_ref, o_ref)

    p_ref = pbuf_ref
    p_tm = p_ref[:, :TM_COLS]
    o = TM_COLS
    b_gate = p_ref[:, o:o + W]
    c_gate = p_ref[:, o + W:o + 2 * W]
    hh = p_ref[:, o + 2 * W:o + 3 * W]
    g2 = p_ref[:, o + 3 * W:o + 4 * W]
    emit_proj = _projection_emitter(xn_ref, npre_ref, sc_ref, sh_ref, win_ref, pbuf_ref)

    u = c_gate * hh
    carry = uc_ref[...]
    conv = (cw_ref[0:1, :] * _shift_rows(u, carry, 2) + cw_ref[1:2, :] * _shift_rows(u, carry, 1) +
            cw_ref[2:3, :] * u)
    uc_ref[...] = u[T - 8:T, :]
    y_ref[:, W:] = (b_gate * conv * _silu(g2)).astype(BF16)
    emit_proj(1)

    prev = _shift_rows(p_tm, prow_ref[...], 1)
    prow_ref[...] = p_tm[T - 8:T, :]
    ps = p_tm + (prev - p_tm) * mu_ref[...]
    emit_proj(2)
    r = ps[:, 0:W]
    k = ps[:, W:2 * W]
    v = ps[:, 2 * W:3 * W]
    g = ps[:, 3 * W:4 * W]
    wd = ps[:, 4 * W:4 * W + LORA]
    ad = ps[:, 4 * W + LORA:4 * W + 2 * LORA]
    w_log = -_softplus(-(w0_ref[...] + _dot(jnp.tanh(wd), w2_ref[...]))) - 0.5
    lw = -jnp.exp(w_log)
    emit_proj(1)
    a_icl = _sigmoid(a0_ref[...] + _dot(ad, a2_ref[...]))
    k2 = k * (1.0 + (a_icl - 1.0) * ka_ref[...])
    ones_bd = ((lax.broadcasted_iota(jnp.int32, (LANES, LANES), 0) >> 6) ==
               (lax.broadcasted_iota(jnp.int32, (LANES, LANES), 1) >> 6)).astype(BF16)
    kk = k * kk_ref[...]
    kk = kk * lax.rsqrt(jnp.maximum(_seg_sum(kk * kk, ones_bd), 1e-24))
    kb = kk * a_icl
    emit_proj(1)

    ti = lax.broadcasted_iota(jnp.int32, (T, T), 0)
    tj = lax.broadcasted_iota(jnp.int32, (T, T), 1)
    tri = ((ti >= tj) & ((ti >> 6) == (tj >> 6))).astype(F32)
    cum_incl = _dot_exact_lhs(tri, lw)
    cum_excl = cum_incl - lw
    rows_of = lambda x, i: x[i:i + 1, :]
    mid_rows = [rows_of(cum_incl, s * L + L // 2 - 1) for s in range(nsub)]
    end_rows = [rows_of(cum_incl, s * L + L - 1) for s in range(nsub)]
    mid = cat([jnp.broadcast_to(m, (L, W)) for m in mid_rows], axis=0)
    end = cat([jnp.broadcast_to(m, (L, W)) for m in end_rows], axis=0)
    g_mid = [jnp.exp(m) for m in mid_rows]
    g_end = [jnp.exp(m) for m in end_rows]
    e_k = jnp.exp(mid - cum_incl)
    e_end = jnp.exp(end - cum_incl)
    emit_proj(1)
    at_all = -kk * jnp.exp(cum_excl - mid)
    rt_all = r * jnp.exp(cum_incl - mid)
    bt_all = kb * e_k
    kt_all = k2 * e_k
    bh_all = kb * e_end
    kh_all = k2 * e_end
    emit_proj(1)

    ri = lax.broadcasted_iota(jnp.int32, (L, LANES), 0)
    cj = lax.broadcasted_iota(jnp.int32, (L, LANES), 1) & (HEAD - 1)
    eye = (ri == cj).astype(F32)
    strict = ri > cj
    same4 = (ri >> 2) == (cj >> 2)
    same16 = (ri >> 4) == (cj >> 4)
    m0 = strict & same4
    m1 = strict & same16 & jnp.logical_not(same4)
    m2 = strict & jnp.logical_not(same16)
    r2 = lax.broadcasted_iota(jnp.int32, (2 * L, 2 * LANES), 0)
    c2 = lax.broadcasted_iota(jnp.int32, (2 * L, 2 * LANES), 1) & (HEAD - 1)
    mask_aa = (r2 & (L - 1)) >= jnp.where(r2 < L, c2 + 1, c2)
    zero_sq = jnp.zeros((LANES, LANES), BF16)

    cs = [(s, pr) for s in range(nsub) for pr in range(HEADS // 2)]
    tile = lambda x, c: x[c[0] * L:(c[0] + 1) * L, c[1] * LANES:(c[1] + 1) * LANES]
    at = [tile(at_all, c) for c in cs]
    rt = [tile(rt_all, c) for c in cs]
    vv = [tile(v, c) for c in cs]
    bh = [tile(bh_all, c) for c in cs]
    kh = [tile(kh_all, c) for c in cs]
    n = range(len(cs))

    aa = [_dot_nt(cat([at[i], rt[i]], axis=0), cat([_bd(tile(bt_all, c)), _bd(tile(kt_all, c))], axis=0))
          for i, c in enumerate(cs)]
    aa = [jnp.where(mask_aa, x, 0.0) for x in aa]
    n_ab = [x[:L, :LANES] for x in aa]
    akv = [_dot(aa[i][:L, LANES:], _bd(vv[i])) for i in n]

    n0 = [jnp.where(m0, x, 0.0) for x in n_ab]
    n0sq = [_dot(x, _bd(x)) for x in n0]
    d0 = [eye + n0[i] + n0sq[i] + _dot(n0[i], _bd(n0sq[i])) for i in n]
    p1 = [_dot(d0[i], _bd(jnp.where(m1, n_ab[i], 0.0))) for i in n]
    p1sq = [_dot(x, _bd(x)) for x in p1]
    e1 = [d0[i] + _dot(p1[i], _bd(d0[i])) for i in n]
    d1 = [e1[i] + _dot(p1sq[i], _bd(e1[i])) for i in n]
    p2 = [_dot(d1[i], _bd(jnp.where(m2, n_ab[i], 0.0))) for i in n]
    z = [_dot(d1[i], cat([_bd(at[i]), _bd(akv[i])], axis=1)) for i in n]
    p2sq = [_dot(x, _bd(x)) for x in p2]
    z = [z[i] + _dot(p2[i], _bd2(z[i])) for i in n]
    z = [z[i] + _dot(p2sq[i], _bd2(z[i])) for i in n]

    top = [_dot(aa[i][L:, :], cat([_bd2(z[i]), cat([zero_sq, _bd(vv[i])], axis=1)], axis=0))
           for i in n]
    gm = [g_mid[c[0]][:, c[1] * LANES:(c[1] + 1) * LANES] for c in cs]
    ge = [g_end[c[0]][:, c[1] * LANES:(c[1] + 1) * LANES] for c in cs]
    q_eff = [(rt[i] + top[i][:, :LANES]) * gm[i] for i in n]
    m_low = [_tn_pair(bh[i], z[i][:, :LANES]) * gm[i] for i in n]
    g_t = [_tn_pair(cat([z[i][:, LANES:], vv[i]], axis=0), cat([bh[i], kh[i]], axis=0)) for i in n]

    state = [ht_ref[pr] for pr in range(HEADS // 2)]
    y_tiles = []
    for i, (s, pr) in enumerate(cs):
        y_tiles.append(_dot_nt(q_eff[i], _bd(state[pr])) + top[i][:, LANES:])
        state[pr] = state[pr] * ge[i] + _dot_nt(state[pr], _bd(m_low[i])) + g_t[i]
    for pr in range(HEADS // 2):
        ht_ref[pr] = state[pr]

    npr = HEADS // 2
    y = cat([cat(y_tiles[s * npr:(s + 1) * npr], axis=1) for s in range(nsub)], axis=0)
    inv_n = 1.0 / HEAD
    yc = y - _seg_sum(y, ones_bd) * inv_n
    emit_proj(1)
    var = _seg_sum(yc * yc, ones_bd) * inv_n
    y_n = yc * lax.rsqrt(var + TM_LN_EPS) * lng_ref[...] + lnb_ref[...]
    emit_proj(1)
    bonus = _seg_sum(r * k2 * rk_ref[...], ones_bd) * v
    y_ref[:, :W] = ((y_n + bonus) * _silu(g)).astype(BF16)
    emit_proj()


def _even_layer(x, scale1, shift, gate, norm_pre, w_in, norm_post, w_out, mu, w0, w2, a0, a2, k_k, k_a, r_k,
                lnx_g, lnx_b, conv_w):
    row = lambda a: a.reshape(1, -1)
    params = [row(mu), row(w0), w2, row(a0), a2, row(k_k), row(k_a), row(r_k), row(lnx_g), row(lnx_b), conv_w]
    scratch = [
        pltpu.VMEM((8, TM_COLS), F32),
        pltpu.VMEM((HEADS // 2, HEAD, LANES), F32),
        pltpu.VMEM((8, WIDTH), F32),
    ]
    return _layer_call(_even_layer_kernel, "even_layer", EVEN_TILE, x, scale1, shift, gate, norm_pre, w_in,
                       norm_post, w_out, params, scratch)


def _odd_layer_kernel(xn_ref, xr_ref, sc_ref, sh_ref, gate_ref, npre_ref, win_ref, npost_ref, wout_ref,
                      cfw_ref, cfb_ref, cfg_ref, cfbb_ref, scw_ref, scb_ref, dtb_ref, alog_ref, dsk_ref, ng_ref,
                      o_ref, pbuf_ref, y_ref, ubuf_ref, xc_ref, st_ref, *, n_tiles):
    L = CHUNK
    T = xn_ref.shape[0]
    nsub = T // L
    W = WIDTH
    cat = jnp.concatenate
    step = pl.program_id(0)

    @pl.when(step == 0)
    def _():
        pbuf_ref[...] = jnp.zeros_like(pbuf_ref)
        y_ref[...] = jnp.zeros_like(y_ref)

    @pl.when((step == 0) | ((step + n_tiles - 1) % n_tiles == 0))
    def _():
        ubuf_ref[...] = jnp.zeros_like(ubuf_ref)
        xc_ref[...] = jnp.zeros_like(xc_ref)
        st_ref[...] = jnp.zeros_like(st_ref)

    _residual_update(y_ref, xr_ref, wout_ref, npost_ref, gate_ref, o_ref)

    p_ref = pbuf_ref
    val = p_ref[:, 0:W]
    glu = p_ref[:, W:2 * W]
    g = p_ref[:, 2 * W:3 * W]
    o = 3 * W
    z = p_ref[:, o:o + W]
    xbc_in = p_ref[:, o + W:o + W + SSD_XBC]
    dt_in = p_ref[:, o + W + SSD_XBC:o + W + SSD_XBC + LANES]
    emit_proj = _projection_emitter(xn_ref, npre_ref, sc_ref, sh_ref, win_ref, pbuf_ref)

    ubuf_ref[CF_HALO:CF_HALO + T, :] = val * _sigmoid(glu)
    first = CF_HALO - (CF_KERNEL - 1)
    acc = jnp.zeros((T, W), F32) + cfb_ref[...]
    proj_blk = None
    for rr in range(8):
        part = None
        for i in range(CF_KERNEL):
            off = first + i
            if off % 8 != rr:
                continue
            w_row = cfw_ref[i:i + 1, :]
            if part is None and proj_blk is not None:
                w_row = w_row + _zero_after(proj_blk, W)
            term = w_row * ubuf_ref[off - rr:off - rr + T + 8, :]
            part = term if part is None else part + term
        if rr == 0:
            acc = acc + part[:T]
        else:
            acc = acc + pltpu.roll(part, T + 8 - rr, 0)[:T]
        proj_blk = emit_proj(1)
    ubuf_ref[:CF_HALO, :] = ubuf_ref[T:T + CF_HALO, :]
    mean = jnp.mean(acc, axis=-1, keepdims=True)
    ac = acc - mean
    var = jnp.mean(ac * ac, axis=-1, keepdims=True)
    ln = ac * lax.rsqrt(var + CF_LN_EPS) * cfg_ref[...] + cfbb_ref[...]
    y_ref[:, :W] = (_silu(ln) * _silu(g)).astype(BF16)
    emit_proj(1)

    carry = xc_ref[...]
    conv = (scw_ref[0:1, :] * _shift_rows(xbc_in, carry, 3) + scw_ref[1:2, :] * _shift_rows(xbc_in, carry, 2) +
            scw_ref[2:3, :] * _shift_rows(xbc_in, carry, 1) + scw_ref[3:4, :] * xbc_in + scb_ref[...])
    xc_ref[...] = xbc_in[T - 8:T, :]
    emit_proj(1)
    xbc = _silu(conv)
    emit_proj(1)
    xs = xbc[:, :W]
    dt = _softplus(dt_in + dtb_ref[...])
    da = dt * (-jnp.exp(alog_ref[...]))
    ti = lax.broadcasted_iota(jnp.int32, (T, T), 0)
    tj = lax.broadcasted_iota(jnp.int32, (T, T), 1)
    tri = ((ti >= tj) & ((ti >> 6) == (tj >> 6))).astype(F32)
    cs = _dot_exact_lhs(tri, da)
    expand = ((lax.broadcasted_iota(jnp.int32, (LANES, W), 1) >> 6) ==
              lax.broadcasted_iota(jnp.int32, (LANES, W), 0)).astype(BF16)

    def widen(x):
        hi, lo = _split_hi_lo(x)
        return (jnp.dot(hi, expand, preferred_element_type=F32) + jnp.dot(lo, expand, preferred_element_type=F32))

    cs_w = widen(cs)
    x_dt = xs * widen(dt)
    ri = lax.broadcasted_iota(jnp.int32, (L, W), 0)
    cj = lax.broadcasted_iota(jnp.int32, (L, W), 1) & (HEAD - 1)
    diag = (ri == cj).astype(F32)
    causal = ri >= cj
    ones_ll = jnp.ones((L, L), F32)

    y_chunks = []
    state = [st_ref[pr] for pr in range(HEADS // 2)]
    for s in range(nsub):
        rows = slice(s * L, (s + 1) * L)
        csw = cs_w[rows]
        last = csw[L - 1:L, :]
        row_b = _dot_exact_lhs(ones_ll, csw * diag)
        lmat = jnp.exp(jnp.where(causal, csw - row_b, -1e30))
        xd = x_dt[rows]
        xd_dec = xd * jnp.exp(last - csw)
        e_cs = jnp.exp(csw)
        e_last = jnp.exp(last)
        tiles = []
        for grp in range(SSD_GROUPS):
            bm = xbc[rows, W + grp * SSD_STATE:W + (grp + 1) * SSD_STATE]
            cm = xbc[rows, W + (SSD_GROUPS + grp) * SSD_STATE:W + (SSD_GROUPS + grp + 1) * SSD_STATE]
            cb2 = _dot_nt(cm, cat([bm, bm], axis=0))
            for e in range(SSD_HPG // 2):
                pr = grp * (SSD_HPG // 2) + e
                lanes = slice(pr * LANES, (pr + 1) * LANES)
                y_diag = _dot(cb2 * lmat[:, lanes], _bd(xd[:, lanes]))
                y_off = _dot(cm, state[pr]) * e_cs[:, lanes]
                state[pr] = state[pr] * e_last[:, lanes] + _dot_tn(bm, xd_dec[:, lanes])
                tiles.append(y_diag + y_off)
        y_chunks.append(cat(tiles, axis=1))
    for pr in range(HEADS // 2):
        st_ref[pr] = state[pr]
    y = cat(y_chunks, axis=0) + xs * dsk_ref[...]
    y = y * _silu(z)
    ms = jnp.mean(y * y, axis=-1, keepdims=True)
    y_ref[:, W:] = (y * lax.rsqrt(ms + NORM_EPS) * ng_ref[...]).astype(BF16)
    emit_proj()


def _odd_layer(x, scale1, shift, gate, norm_pre, w_in, norm_post, w_out, cf_w, cf_b, cf_g, cf_bb, sc_w, sc_b,
               dt_bias, a_log, d_skip, norm_g):
    row = lambda a: a.reshape(1, -1)
    pad_lanes = lambda a: jnp.zeros((1, LANES), F32).at[0, :a.shape[0]].set(a)
    params = [cf_w, row(cf_b), row(cf_g), row(cf_bb), sc_w, row(sc_b), pad_lanes(dt_bias), pad_lanes(a_log),
              row(jnp.repeat(d_skip, HEAD)), row(norm_g)]
    scratch = [
        pltpu.VMEM((CF_HALO + ODD_TILE + 8, WIDTH), F32),
        pltpu.VMEM((8, SSD_XBC), F32),
        pltpu.VMEM((HEADS // 2, SSD_STATE, LANES), F32),
    ]
    return _layer_call(_odd_layer_kernel, "odd_layer", ODD_TILE, x, scale1, shift, gate, norm_pre, w_in,
                       norm_post, w_out, params, scratch)


def kernel(x, c, ada_w, ada_b, norm_pre, norm_post, ev_w_in, ev_w_out, tm_mu, tm_w0, tm_w2, tm_a0, tm_a2,
           tm_k_k, tm_k_a, tm_r_k, tm_lnx_g, tm_lnx_b, sc_conv_w, od_w_in, od_w_out, cf_conv_w, cf_conv_b,
           cf_ln_g, cf_ln_b, ssd_conv_w, ssd_conv_b, ssd_dt_bias, ssd_a_log, ssd_d, ssd_norm_g):
    depth = ada_w.shape[0]
    bsz, t, d = x.shape
    assert d == D_MODEL and t % EVEN_TILE == 0 and t % ODD_TILE == 0
    mod = _ada_modulation(c, ada_w, ada_b)
    od_cols = od_w_in.shape[-1]
    od_w_pad = jnp.zeros(od_w_in.shape[:2] + (ODD_COLS_PAD,), BF16).at[..., :od_cols].set(od_w_in.astype(BF16))
    for i in range(depth):
        shift = mod[i, :, None, 0:d]
        scale1 = 1.0 + mod[i, :, None, d:2 * d]
        gate = mod[i, :, None, 2 * d:3 * d]
        j = i // 2
        if i % 2 == 0:
            x = _even_layer(x, scale1, shift, gate, norm_pre[i], ev_w_in[j].astype(BF16), norm_post[i],
                            ev_w_out[j].astype(BF16), tm_mu[j], tm_w0[j], tm_w2[j], tm_a0[j], tm_a2[j],
                            tm_k_k[j].reshape(-1), tm_k_a[j].reshape(-1), tm_r_k[j].reshape(-1), tm_lnx_g[j],
                            tm_lnx_b[j], sc_conv_w[j])
        else:
            x = _odd_layer(x, scale1, shift, gate, norm_pre[i], od_w_pad[j], norm_post[i],
                           od_w_out[j].astype(BF16), cf_conv_w[j], cf_conv_b[j], cf_ln_g[j], cf_ln_b[j],
                           ssd_conv_w[j], ssd_conv_b[j], ssd_dt_bias[j], ssd_a_log[j], ssd_d[j], ssd_norm_g[j])
    return x
```

```python
import functools

import jax
import jax.numpy as jnp
from jax import lax
from jax.experimental import pallas as pl
from jax.experimental.pallas import tpu as pltpu

F32 = jnp.float32
BF16 = jnp.bfloat16

D_MODEL = 1024
CHUNK = 64
EVEN_TILE = 512
ODD_TILE = 512
HEAD = 64
HEADS = 8
WIDTH = 512
LORA = 64
TM_COLS = 4 * WIDTH + 2 * LORA
EVEN_COLS = TM_COLS + 4 * WIDTH
SSD_STATE = 128
SSD_GROUPS = 2
SSD_HPG = HEADS // SSD_GROUPS
SSD_XBC = WIDTH + 2 * SSD_GROUPS * SSD_STATE
CF_KERNEL = 31
CF_HALO = 32
LANES = 128
ODD_COLS_PAD = 3 * WIDTH + WIDTH + SSD_XBC + LANES
NORM_EPS = 1e-6
TM_LN_EPS = 64e-5
CF_LN_EPS = 1e-5
VMEM_LIMIT = 56 * 1024 * 1024


def _dot(a, b):
    return jnp.dot(a.astype(BF16), b.astype(BF16), preferred_element_type=F32)


def _dot_nt(a, b):
    return lax.dot_general(a.astype(BF16), b.astype(BF16), (((1,), (1,)), ((), ())),
                           preferred_element_type=F32)


def _dot_tn(a, b):
    return lax.dot_general(a.astype(BF16), b.astype(BF16), (((0,), (0,)), ((), ())),
                           preferred_element_type=F32)


def _split_hi_lo(x):
    hi = x.astype(BF16)
    lo = (x - hi.astype(F32)).astype(BF16)
    return hi, lo


def _dot_exact_lhs(m, x):
    hi, lo = _split_hi_lo(x)
    mb = m.astype(BF16)
    return (jnp.dot(mb, hi, preferred_element_type=F32) +
            jnp.dot(mb, lo, preferred_element_type=F32))


def _sigmoid(x):
    return 0.5 * jnp.tanh(0.5 * x) + 0.5


def _silu(x):
    return x * _sigmoid(x)


def _softplus(x):
    return jnp.maximum(x, 0.0) + jnp.log(1.0 + jnp.exp(-jnp.abs(x)))


def _shift_rows(x, carry, d):
    out = pltpu.roll(x, d, 0)
    row = lax.broadcasted_iota(jnp.int32, (8, 1), 0)
    head = jnp.where(row < d, pltpu.roll(carry, d, 0), out[:8])
    return jnp.concatenate([head, out[8:]], axis=0)


def _ada_kernel(c_ref, w_ref, b_ref, o_ref):
    ca = _silu(c_ref[...])
    o_ref[...] = jnp.dot(ca, w_ref[...], preferred_element_type=F32,
                         precision=lax.Precision.HIGHEST) + b_ref[...]


def _ada_modulation(c, ada_w, ada_b):
    depth, d, d3 = ada_w.shape
    bsz = c.shape[0]
    rows = 8
    c_pad = jnp.zeros((rows, d), F32).at[:bsz].set(c)
    tn = 768
    out = pl.pallas_call(
        _ada_kernel,
        grid=(depth, d3 // tn),
        in_specs=[
            pl.BlockSpec((rows, d), lambda i, j: (0, 0)),
            pl.BlockSpec((None, d, tn), lambda i, j: (i, 0, j)),
            pl.BlockSpec((None, 1, tn), lambda i, j: (i, 0, j)),
        ],
        out_specs=pl.BlockSpec((None, rows, tn), lambda i, j: (i, 0, j)),
        out_shape=jax.ShapeDtypeStruct((depth, rows, d3), F32),
        compiler_params=pltpu.CompilerParams(
            dimension_semantics=("parallel", "parallel"), vmem_limit_bytes=VMEM_LIMIT),
        name="ada_modulation",
    )(c_pad, ada_w, ada_b.reshape(depth, 1, d3))
    return out[:, :bsz]


PIPE_LAG = 2


PROJ_BLOCK = 3 * LANES


def _projection_emitter(xn_ref, npre_ref, sc_ref, sh_ref, win_ref, p_ref, block=PROJ_BLOCK):
    x = xn_ref[...]
    ms = jnp.mean(x * x, axis=-1, keepdims=True)
    h = x * lax.rsqrt(ms + NORM_EPS) * npre_ref[...]
    h = (h * sc_ref[...] + sh_ref[...]).astype(BF16)
    cols = win_ref.shape[1]
    starts = list(range(0, cols, block))

    def emit(k=None):
        blk = None
        for _ in range(len(starts) if k is None else min(k, len(starts))):
            lo = starts.pop(0)
            hi = min(lo + block, cols)
            blk = jnp.dot(h, win_ref[:, lo:hi], preferred_element_type=F32)
            p_ref[:, lo:hi] = blk
        return blk

    return emit


def _zero_after(dep, width):
    bits = lax.bitcast_convert_type(dep[0:8, 0:LANES], jnp.int32)
    zero = lax.shift_right_logical(lax.shift_right_logical(bits, 16), 16).astype(F32)
    return jnp.concatenate([zero[0:1, :]] * (width // LANES), axis=1)


def _residual_update(y_ref, x_ref, wout_ref, npost_ref, gate_ref, o_ref):
    yo = jnp.dot(y_ref[...], wout_ref[...], preferred_element_type=F32)
    ms = jnp.mean(yo * yo, axis=-1, keepdims=True)
    o_ref[...] = x_ref[...] + gate_ref[...] * (yo * lax.rsqrt(ms + NORM_EPS) * npost_ref[...])


def _layer_call(body, name, tile, x, scale1, shift, gate, norm_pre, w_in, norm_post, w_out, params, scratch):
    bsz, t, d = x.shape
    n_tiles = t // tile
    last = bsz * n_tiles - 1
    cols = w_in.shape[1]
    row = lambda a: a.reshape(1, -1)
    full = lambda a: pl.BlockSpec(a.shape, lambda s: (0,) * a.ndim)
    tile_in = lambda s: jnp.minimum(s, last)
    tile_out = lambda s: jnp.maximum(s - PIPE_LAG, 0)
    xn_spec = pl.BlockSpec((tile, d), lambda s: (tile_in(s), 0))
    xr_spec = pl.BlockSpec((tile, d), lambda s: (tile_out(s), 0))
    mod_in = pl.BlockSpec((None, 1, d), lambda s: (tile_in(s) // n_tiles, 0, 0))
    mod_out = pl.BlockSpec((None, 1, d), lambda s: (tile_out(s) // n_tiles, 0, 0))
    shared = [row(norm_pre), w_in, row(norm_post), w_out] + list(params)
    x2 = x.reshape(bsz * t, d)
    out = pl.pallas_call(
        functools.partial(body, n_tiles=n_tiles),
        grid=(bsz * n_tiles + PIPE_LAG,),
        in_specs=[xn_spec, xr_spec, mod_in, mod_in, mod_out] + [full(a) for a in shared],
        out_specs=xr_spec,
        out_shape=jax.ShapeDtypeStruct((bsz * t, d), F32),
        scratch_shapes=[
            pltpu.VMEM((tile, cols), F32),
            pltpu.VMEM((tile, 2 * WIDTH), BF16),
        ] + list(scratch),
        compiler_params=pltpu.CompilerParams(
            dimension_semantics=("arbitrary",), vmem_limit_bytes=VMEM_LIMIT),
        name=name,
    )(x2, x2, scale1, shift, gate, *shared)
    return out.reshape(bsz, t, d)


def _bd(x):
    lane = lax.broadcasted_iota(jnp.int32, x.shape, 1)
    return jnp.concatenate([jnp.where(lane < HEAD, x, 0.0), jnp.where(lane >= HEAD, x, 0.0)],
                           axis=0).astype(BF16)


def _bd2(z):
    return jnp.concatenate([_bd(z[:, :LANES]), _bd(z[:, LANES:])], axis=1)


def _tn_pair(x, y):
    full = _dot_tn(x, y)
    lane = lax.broadcasted_iota(jnp.int32, (HEAD, LANES), 1)
    return jnp.where(lane < HEAD, full[:HEAD], full[HEAD:])


def _seg_sum(x, ones_bd):
    rows = x.shape[0]
    n = x.shape[1] // LANES
    xs = jnp.concatenate([x[:, j * LANES:(j + 1) * LANES] for j in range(n)], axis=0)
    s = jnp.dot(xs.astype(BF16), ones_bd, preferred_element_type=F32)
    return jnp.concatenate([s[j * rows:(j + 1) * rows] for j in range(n)], axis=1)


def _even_layer_kernel(xn_ref, xr_ref, sc_ref, sh_ref, gate_ref, npre_ref, win_ref, npost_ref, wout_ref,
                       mu_ref, w0_ref, w2_ref, a0_ref, a2_ref, kk_ref, ka_ref, rk_ref, lng_ref, lnb_ref, cw_ref,
                       o_ref, pbuf_ref, y_ref, prow_ref, ht_ref, uc_ref, *, n_tiles):
    L = CHUNK
    T = xn_ref.shape[0]
    nsub = T // L
    W = WIDTH
    cat = jnp.concatenate
    step = pl.program_id(0)

    @pl.when(step == 0)
    def _():
        pbuf_ref[...] = jnp.zeros_like(pbuf_ref)
        y_ref[...] = jnp.zeros_like(y_ref)

    @pl.when((step == 0) | ((step + n_tiles - 1) % n_tiles == 0))
    def _():
        prow_ref[...] = jnp.zeros_like(prow_ref)
        ht_ref[...] = jnp.zeros_like(ht_ref)
        uc_ref[...] = jnp.zeros_like(uc_ref)

    _residual_update(y_ref, xr_ref, wout_ref, npost_ref, gate_ref, o_ref)

    p_ref = pbuf_ref
    p_tm = p_ref[:, :TM_COLS]
    o = TM_COLS
    b_gate = p_ref[:, o:o + W]
    c_gate = p_ref[:, o + W:o + 2 * W]
    hh = p_ref[:, o + 2 * W:o + 3 * W]
    g2 = p_ref[:, o + 3 * W:o + 4 * W]
    emit_proj = _projection_emitter(xn_ref, npre_ref, sc_ref, sh_ref, win_ref, pbuf_ref, block=33 * LANES)

    u = c_gate * hh
    carry = uc_ref[...]
    conv = (cw_ref[0:1, :] * _shift_rows(u, carry, 2) + cw_ref[1:2, :] * _shift_rows(u, carry, 1) +
            cw_ref[2:3, :] * u)
    uc_ref[...] = u[T - 8:T, :]
    y_ref[:, W:] = (b_gate * conv * _silu(g2)).astype(BF16)
    emit_proj(1)

    prev = _shift_rows(p_tm, prow_ref[...], 1)
    prow_ref[...] = p_tm[T - 8:T, :]
    ps = p_tm + (prev - p_tm) * mu_ref[...]
    emit_proj(2)
    r = ps[:, 0:W]
    k = ps[:, W:2 * W]
    v = ps[:, 2 * W:3 * W]
    g = ps[:, 3 * W:4 * W]
    wd = ps[:, 4 * W:4 * W + LORA]
    ad = ps[:, 4 * W + LORA:4 * W + 2 * LORA]
    w_log = -_softplus(-(w0_ref[...] + _dot(jnp.tanh(wd), w2_ref[...]))) - 0.5
    lw = -jnp.exp(w_log)
    emit_proj(1)
    a_icl = _sigmoid(a0_ref[...] + _dot(ad, a2_ref[...]))
    k2 = k * (1.0 + (a_icl - 1.0) * ka_ref[...])
    ones_bd = ((lax.broadcasted_iota(jnp.int32, (LANES, LANES), 0) >> 6) ==
               (lax.broadcasted_iota(jnp.int32, (LANES, LANES), 1) >> 6)).astype(BF16)
    kk = k * kk_ref[...]
    kk = kk * lax.rsqrt(jnp.maximum(_seg_sum(kk * kk, ones_bd), 1e-24))
    kb = kk * a_icl
    emit_proj(1)

    ti = lax.broadcasted_iota(jnp.int32, (T, T), 0)
    tj = lax.broadcasted_iota(jnp.int32, (T, T), 1)
    tri = ((ti >= tj) & ((ti >> 6) == (tj >> 6))).astype(F32)
    cum_incl = _dot_exact_lhs(tri, lw)
    cum_excl = cum_incl - lw
    rows_of = lambda x, i: x[i:i + 1, :]
    mid_rows = [rows_of(cum_incl, s * L + L // 2 - 1) for s in range(nsub)]
    end_rows = [rows_of(cum_incl, s * L + L - 1) for s in range(nsub)]
    mid = cat([jnp.broadcast_to(m, (L, W)) for m in mid_rows], axis=0)
    end = cat([jnp.broadcast_to(m, (L, W)) for m in end_rows], axis=0)
    g_mid = [jnp.exp(m) for m in mid_rows]
    g_end = [jnp.exp(m) for m in end_rows]
    e_k = jnp.exp(mid - cum_incl)
    e_end = jnp.exp(end - cum_incl)
    emit_proj(1)
    at_all = -kk * jnp.exp(cum_excl - mid)
    rt_all = r * jnp.exp(cum_incl - mid)
    bt_all = kb * e_k
    kt_all = k2 * e_k
    bh_all = kb * e_end
    kh_all = k2 * e_end
    emit_proj(1)

    ri = lax.broadcasted_iota(jnp.int32, (L, LANES), 0)
    cj = lax.broadcasted_iota(jnp.int32, (L, LANES), 1) & (HEAD - 1)
    eye = (ri == cj).astype(F32)
    strict = ri > cj
    same4 = (ri >> 2) == (cj >> 2)
    same16 = (ri >> 4) == (cj >> 4)
    m0 = strict & same4
    m1 = strict & same16 & jnp.logical_not(same4)
    m2 = strict & jnp.logical_not(same16)
    r2 = lax.broadcasted_iota(jnp.int32, (2 * L, 2 * LANES), 0)
    c2 = lax.broadcasted_iota(jnp.int32, (2 * L, 2 * LANES), 1) & (HEAD - 1)
    mask_aa = (r2 & (L - 1)) >= jnp.where(r2 < L, c2 + 1, c2)
    zero_sq = jnp.zeros((LANES, LANES), BF16)

    cs = [(s, pr) for s in range(nsub) for pr in range(HEADS // 2)]
    tile = lambda x, c: x[c[0] * L:(c[0] + 1) * L, c[1] * LANES:(c[1] + 1) * LANES]
    at = [tile(at_all, c) for c in cs]
    rt = [tile(rt_all, c) for c in cs]
    vv = [tile(v, c) for c in cs]
    bh = [tile(bh_all, c) for c in cs]
    kh = [tile(kh_all, c) for c in cs]
    n = range(len(cs))

    aa = [_dot_nt(cat([at[i], rt[i]], axis=0), cat([_bd(tile(bt_all, c)), _bd(tile(kt_all, c))], axis=0))
          for i, c in enumerate(cs)]
    aa = [jnp.where(mask_aa, x, 0.0) for x in aa]
    n_ab = [x[:L, :LANES] for x in aa]
    akv = [_dot(aa[i][:L, LANES:], _bd(vv[i])) for i in n]

    n0 = [jnp.where(m0, x, 0.0) for x in n_ab]
    n0sq = [_dot(x, _bd(x)) for x in n0]
    d0 = [eye + n0[i] + n0sq[i] + _dot(n0[i], _bd(n0sq[i])) for i in n]
    p1 = [_dot(d0[i], _bd(jnp.where(m1, n_ab[i], 0.0))) for i in n]
    pm = [_dot(p1[i], cat([_bd(p1[i]), _bd(d0[i])], axis=1)) for i in n]
    e1 = [d0[i] + pm[i][:, LANES:] for i in n]
    d1 = [e1[i] + _dot(pm[i][:, :LANES], _bd(e1[i])) for i in n]
    dm = [_dot(d1[i], cat([_bd(jnp.where(m2, n_ab[i], 0.0)), _bd(at[i]), _bd(akv[i])], axis=1))
          for i in n]
    p2 = [x[:, :LANES] for x in dm]
    z = [x[:, LANES:] for x in dm]
    qm = [_dot(p2[i], cat([_bd(p2[i]), _bd2(z[i])], axis=1)) for i in n]
    z = [z[i] + qm[i][:, LANES:] for i in n]
    z = [z[i] + _dot(qm[i][:, :LANES], _bd2(z[i])) for i in n]

    top = [_dot(aa[i][L:, :], cat([_bd2(z[i]), cat([zero_sq, _bd(vv[i])], axis=1)], axis=0))
           for i in n]
    gm = [g_mid[c[0]][:, c[1] * LANES:(c[1] + 1) * LANES] for c in cs]
    ge = [g_end[c[0]][:, c[1] * LANES:(c[1] + 1) * LANES] for c in cs]
    q_eff = [(rt[i] + top[i][:, :LANES]) * gm[i] for i in n]
    m_low = [_tn_pair(bh[i], z[i][:, :LANES]) * gm[i] for i in n]
    g_t = [_tn_pair(cat([z[i][:, LANES:], vv[i]], axis=0), cat([bh[i], kh[i]], axis=0)) for i in n]

    state = [ht_ref[pr] for pr in range(HEADS // 2)]
    y_tiles = []
    for i, (s, pr) in enumerate(cs):
        y_tiles.append(_dot_nt(q_eff[i], _bd(state[pr])) + top[i][:, LANES:])
        state[pr] = state[pr] * ge[i] + _dot_nt(state[pr], _bd(m_low[i])) + g_t[i]
    for pr in range(HEADS // 2):
        ht_ref[pr] = state[pr]

    npr = HEADS // 2
    y = cat([cat(y_tiles[s * npr:(s + 1) * npr], axis=1) for s in range(nsub)], axis=0)
    inv_n = 1.0 / HEAD
    yc = y - _seg_sum(y, ones_bd) * inv_n
    emit_proj(1)
    var = _seg_sum(yc * yc, ones_bd) * inv_n
    y_n = yc * lax.rsqrt(var + TM_LN_EPS) * lng_ref[...] + lnb_ref[...]
    emit_proj(1)
    bonus = _seg_sum(r * k2 * rk_ref[...], ones_bd) * v
    y_ref[:, :W] = ((y_n + bonus) * _silu(g)).astype(BF16)
    emit_proj()


def _even_layer(x, scale1, shift, gate, norm_pre, w_in, norm_post, w_out, mu, w0, w2, a0, a2, k_k, k_a, r_k,
                lnx_g, lnx_b, conv_w):
    row = lambda a: a.reshape(1, -1)
    params = [row(mu), row(w0), w2, row(a0), a2, row(k_k), row(k_a), row(r_k), row(lnx_g), row(lnx_b), conv_w]
    scratch = [
        pltpu.VMEM((8, TM_COLS), F32),
        pltpu.VMEM((HEADS // 2, HEAD, LANES), F32),
        pltpu.VMEM((8, WIDTH), F32),
    ]
    return _layer_call(_even_layer_kernel, "even_layer", EVEN_TILE, x, scale1, shift, gate, norm_pre, w_in,
                       norm_post, w_out, params, scratch)


def _odd_layer_kernel(xn_ref, xr_ref, sc_ref, sh_ref, gate_ref, npre_ref, win_ref, npost_ref, wout_ref,
                      cfw_ref, cfb_ref, cfg_ref, cfbb_ref, scw_ref, scb_ref, dtb_ref, alog_ref, dsk_ref, ng_ref,
                      o_ref, pbuf_ref, y_ref, ubuf_ref, xc_ref, st_ref, *, n_tiles):
    L = CHUNK
    T = xn_ref.shape[0]
    nsub = T // L
    W = WIDTH
    cat = jnp.concatenate
    step = pl.program_id(0)

    @pl.when(step == 0)
    def _():
        pbuf_ref[...] = jnp.zeros_like(pbuf_ref)
        y_ref[...] = jnp.zeros_like(y_ref)

    @pl.when((step == 0) | ((step + n_tiles - 1) % n_tiles == 0))
    def _():
        ubuf_ref[...] = jnp.zeros_like(ubuf_ref)
        xc_ref[...] = jnp.zeros_like(xc_ref)
        st_ref[...] = jnp.zeros_like(st_ref)

    _residual_update(y_ref, xr_ref, wout_ref, npost_ref, gate_ref, o_ref)

    p_ref = pbuf_ref
    val = p_ref[:, 0:W]
    glu = p_ref[:, W:2 * W]
    g = p_ref[:, 2 * W:3 * W]
    o = 3 * W
    z = p_ref[:, o:o + W]
    xbc_in = p_ref[:, o + W:o + W + SSD_XBC]
    dt_in = p_ref[:, o + W + SSD_XBC:o + W + SSD_XBC + LANES]
    emit_proj = _projection_emitter(xn_ref, npre_ref, sc_ref, sh_ref, win_ref, pbuf_ref, block=4 * LANES)

    ubuf_ref[CF_HALO:CF_HALO + T, :] = val * _sigmoid(glu)
    first = CF_HALO - (CF_KERNEL - 1)
    acc = jnp.zeros((T, W), F32) + cfb_ref[...]
    proj_blk = None
    for rr in range(8):
        part = None
        for i in range(CF_KERNEL):
            off = first + i
            if off % 8 != rr:
                continue
            w_row = cfw_ref[i:i + 1, :]
            if part is None and proj_blk is not None:
                w_row = w_row + _zero_after(proj_blk, W)
            term = w_row * ubuf_ref[off - rr:off - rr + T + 8, :]
            part = term if part is None else part + term
        if rr == 0:
            acc = acc + part[:T]
        else:
            acc = acc + pltpu.roll(part, T + 8 - rr, 0)[:T]
        proj_blk = emit_proj(1)
    ubuf_ref[:CF_HALO, :] = ubuf_ref[T:T + CF_HALO, :]
    mean = jnp.mean(acc, axis=-1, keepdims=True)
    ac = acc - mean
    var = jnp.mean(ac * ac, axis=-1, keepdims=True)
    ln = ac * lax.rsqrt(var + CF_LN_EPS) * cfg_ref[...] + cfbb_ref[...]
    y_ref[:, :W] = (_silu(ln) * _silu(g)).astype(BF16)
    emit_proj(1)

    carry = xc_ref[...]
    conv = (scw_ref[0:1, :] * _shift_rows(xbc_in, carry, 3) + scw_ref[1:2, :] * _shift_rows(xbc_in, carry, 2) +
            scw_ref[2:3, :] * _shift_rows(xbc_in, carry, 1) + scw_ref[3:4, :] * xbc_in + scb_ref[...])
    xc_ref[...] = xbc_in[T - 8:T, :]
    emit_proj(1)
    xbc = _silu(conv)
    emit_proj(1)
    xs = xbc[:, :W]
    dt = _softplus(dt_in + dtb_ref[...])
    da = dt * (-jnp.exp(alog_ref[...]))
    ti = lax.broadcasted_iota(jnp.int32, (T, T), 0)
    tj = lax.broadcasted_iota(jnp.int32, (T, T), 1)
    tri = ((ti >= tj) & ((ti >> 6) == (tj >> 6))).astype(F32)
    cs = _dot_exact_lhs(tri, da)
    expand = ((lax.broadcasted_iota(jnp.int32, (LANES, W), 1) >> 6) ==
              lax.broadcasted_iota(jnp.int32, (LANES, W), 0)).astype(BF16)

    def widen(x):
        hi, lo = _split_hi_lo(x)
        return (jnp.dot(hi, expand, preferred_element_type=F32) + jnp.dot(lo, expand, preferred_element_type=F32))

    cs_w = widen(cs)
    x_dt = xs * widen(dt)
    ri = lax.broadcasted_iota(jnp.int32, (L, W), 0)
    cj = lax.broadcasted_iota(jnp.int32, (L, W), 1) & (HEAD - 1)
    diag = (ri == cj).astype(F32)
    causal = ri >= cj
    ones_ll = jnp.ones((L, L), F32)

    y_chunks = []
    state = [st_ref[pr] for pr in range(HEADS // 2)]
    for s in range(nsub):
        rows = slice(s * L, (s + 1) * L)
        csw = cs_w[rows]
        last = csw[L - 1:L, :]
        row_b = _dot_exact_lhs(ones_ll, csw * diag)
        lmat = jnp.exp(jnp.where(causal, csw - row_b, -1e30))
        xd = x_dt[rows]
        xd_dec = xd * jnp.exp(last - csw)
        e_cs = jnp.exp(csw)
        e_last = jnp.exp(last)
        tiles = []
        for grp in range(SSD_GROUPS):
            bm = xbc[rows, W + grp * SSD_STATE:W + (grp + 1) * SSD_STATE]
            cm = xbc[rows, W + (SSD_GROUPS + grp) * SSD_STATE:W + (SSD_GROUPS + grp + 1) * SSD_STATE]
            cb2 = _dot_nt(cm, cat([bm, bm], axis=0))
            for e in range(SSD_HPG // 2):
                pr = grp * (SSD_HPG // 2) + e
                lanes = slice(pr * LANES, (pr + 1) * LANES)
                y_diag = _dot(cb2 * lmat[:, lanes], _bd(xd[:, lanes]))
                y_off = _dot(cm, state[pr]) * e_cs[:, lanes]
                state[pr] = state[pr] * e_last[:, lanes] + _dot_tn(bm, xd_dec[:, lanes])
                tiles.append(y_diag + y_off)
        y_chunks.append(cat(tiles, axis=1))
    for pr in range(HEADS // 2):
        st_ref[pr] = state[pr]
    y = cat(y_chunks, axis=0) + xs * dsk_ref[...]
    y = y * _silu(z)
    ms = jnp.mean(y * y, axis=-1, keepdims=True)
    y_ref[:, W:] = (y * lax.rsqrt(ms + NORM_EPS) * ng_ref[...]).astype(BF16)
    emit_proj()


def _odd_layer(x, scale1, shift, gate, norm_pre, w_in, norm_post, w_out, cf_w, cf_b, cf_g, cf_bb, sc_w, sc_b,
               dt_bias, a_log, d_skip, norm_g):
    row = lambda a: a.reshape(1, -1)
    pad_lanes = lambda a: jnp.zeros((1, LANES), F32).at[0, :a.shape[0]].set(a)
    params = [cf_w, row(cf_b), row(cf_g), row(cf_bb), sc_w, row(sc_b), pad_lanes(dt_bias), pad_lanes(a_log),
              row(jnp.repeat(d_skip, HEAD)), row(norm_g)]
    scratch = [
        pltpu.VMEM((CF_HALO + ODD_TILE + 8, WIDTH), F32),
        pltpu.VMEM((8, SSD_XBC), F32),
        pltpu.VMEM((HEADS // 2, SSD_STATE, LANES), F32),
    ]
    return _layer_call(_odd_layer_kernel, "odd_layer", ODD_TILE, x, scale1, shift, gate, norm_pre, w_in,
                       norm_post, w_out, params, scratch)


def kernel(x, c, ada_w, ada_b, norm_pre, norm_post, ev_w_in, ev_w_out, tm_mu, tm_w0, tm_w2, tm_a0, tm_a2,
           tm_k_k, tm_k_a, tm_r_k, tm_lnx_g, tm_lnx_b, sc_conv_w, od_w_in, od_w_out, cf_conv_w, cf_conv_b,
           cf_ln_g, cf_ln_b, ssd_conv_w, ssd_conv_b, ssd_dt_bias, ssd_a_log, ssd_d, ssd_norm_g):
    depth = ada_w.shape[0]
    bsz, t, d = x.shape
    assert d == D_MODEL and t % EVEN_TILE == 0 and t % ODD_TILE == 0
    mod = _ada_modulation(c, ada_w, ada_b)
    od_cols = od_w_in.shape[-1]
    od_w_pad = jnp.zeros(od_w_in.shape[:2] + (ODD_COLS_PAD,), BF16).at[..., :od_cols].set(od_w_in.astype(BF16))
    for i in range(depth):
        shift = mod[i, :, None, 0:d]
        scale1 = 1.0 + mod[i, :, None, d:2 * d]
        gate = mod[i, :, None, 2 * d:3 * d]
        j = i // 2
        if i % 2 == 0:
            x = _even_layer(x, scale1, shift, gate, norm_pre[i], ev_w_in[j].astype(BF16), norm_post[i],
                            ev_w_out[j].astype(BF16), tm_mu[j], tm_w0[j], tm_w2[j], tm_a0[j], tm_a2[j],
                            tm_k_k[j].reshape(-1), tm_k_a[j].reshape(-1), tm_r_k[j].reshape(-1), tm_lnx_g[j],
                            tm_lnx_b[j], sc_conv_w[j])
        else:
            x = _odd_layer(x, scale1, shift, gate, norm_pre[i], od_w_pad[j], norm_post[i],
                           od_w_out[j].astype(BF16), cf_conv_w[j], cf_conv_b[j], cf_ln_g[j], cf_ln_b[j],
                           ssd_conv_w[j], ssd_conv_b[j], ssd_dt_bias[j], ssd_a_log[j], ssd_d[j], ssd_norm_g[j])
    return x
```

```python
import functools

import jax
import jax.numpy as jnp
from jax import lax
from jax.experimental import pallas as pl
from jax.experimental.pallas import tpu as pltpu

F32 = jnp.float32
BF16 = jnp.bfloat16

D_MODEL = 1024
CHUNK = 64
EVEN_TILE = 512
ODD_TILE = 512
HEAD = 64
HEADS = 8
WIDTH = 512
LORA = 64
TM_COLS = 4 * WIDTH + 2 * LORA
EVEN_COLS = TM_COLS + 4 * WIDTH
SSD_STATE = 128
SSD_GROUPS = 2
SSD_HPG = HEADS // SSD_GROUPS
SSD_XBC = WIDTH + 2 * SSD_GROUPS * SSD_STATE
CF_KERNEL = 31
CF_HALO = 32
LANES = 128
ODD_COLS_PAD = 3 * WIDTH + WIDTH + SSD_XBC + LANES
NORM_EPS = 1e-6
TM_LN_EPS = 64e-5
CF_LN_EPS = 1e-5
VMEM_LIMIT = 56 * 1024 * 1024


def _dot(a, b):
    return jnp.dot(a.astype(BF16), b.astype(BF16), preferred_element_type=F32)


def _dot_nt(a, b):
    return lax.dot_general(a.astype(BF16), b.astype(BF16), (((1,), (1,)), ((), ())),
                           preferred_element_type=F32)


def _dot_tn(a, b):
    return lax.dot_general(a.astype(BF16), b.astype(BF16), (((0,), (0,)), ((), ())),
                           preferred_element_type=F32)


def _split_hi_lo(x):
    hi = x.astype(BF16)
    lo = (x - hi.astype(F32)).astype(BF16)
    return hi, lo


def _dot_exact_lhs(m, x):
    hi, lo = _split_hi_lo(x)
    mb = m.astype(BF16)
    return (jnp.dot(mb, hi, preferred_element_type=F32) +
            jnp.dot(mb, lo, preferred_element_type=F32))


def _sigmoid(x):
    return 0.5 * jnp.tanh(0.5 * x) + 0.5


def _silu(x):
    return x * _sigmoid(x)


def _softplus(x):
    return jnp.maximum(x, 0.0) + jnp.log(1.0 + jnp.exp(-jnp.abs(x)))


def _shift_rows(x, carry, d):
    out = pltpu.roll(x, d, 0)
    row = lax.broadcasted_iota(jnp.int32, (8, 1), 0)
    head = jnp.where(row < d, pltpu.roll(carry, d, 0), out[:8])
    return jnp.concatenate([head, out[8:]], axis=0)


def _ada_kernel(c_ref, w_ref, b_ref, o_ref):
    ca = _silu(c_ref[...])
    o_ref[...] = jnp.dot(ca, w_ref[...], preferred_element_type=F32,
                         precision=lax.Precision.HIGHEST) + b_ref[...]


def _ada_modulation(c, ada_w, ada_b):
    depth, d, d3 = ada_w.shape
    bsz = c.shape[0]
    rows = 8
    c_pad = jnp.zeros((rows, d), F32).at[:bsz].set(c)
    tn = 768
    out = pl.pallas_call(
        _ada_kernel,
        grid=(depth, d3 // tn),
        in_specs=[
            pl.BlockSpec((rows, d), lambda i, j: (0, 0)),
            pl.BlockSpec((None, d, tn), lambda i, j: (i, 0, j)),
            pl.BlockSpec((None, 1, tn), lambda i, j: (i, 0, j)),
        ],
        out_specs=pl.BlockSpec((None, rows, tn), lambda i, j: (i, 0, j)),
        out_shape=jax.ShapeDtypeStruct((depth, rows, d3), F32),
        compiler_params=pltpu.CompilerParams(
            dimension_semantics=("parallel", "parallel"), vmem_limit_bytes=VMEM_LIMIT),
        name="ada_modulation",
    )(c_pad, ada_w, ada_b.reshape(depth, 1, d3))
    return out[:, :bsz]


PIPE_LAG = 2


PROJ_BLOCK = 3 * LANES


def _projection_emitter(xn_ref, npre_ref, sc_ref, sh_ref, win_ref, p_ref, block=PROJ_BLOCK):
    x = xn_ref[...]
    ms = jnp.mean(x * x, axis=-1, keepdims=True)
    h = x * lax.rsqrt(ms + NORM_EPS) * npre_ref[...]
    h = (h * sc_ref[...] + sh_ref[...]).astype(BF16)
    cols = win_ref.shape[1]
    starts = list(range(0, cols, block))

    def emit(k=None):
        blk = None
        for _ in range(len(starts) if k is None else min(k, len(starts))):
            lo = starts.pop(0)
            hi = min(lo + block, cols)
            blk = jnp.dot(h, win_ref[:, lo:hi], preferred_element_type=F32)
            p_ref[:, lo:hi] = blk
        return blk

    return emit


def _zero_after(dep, width):
    bits = lax.bitcast_convert_type(dep[0:8, 0:LANES], jnp.int32)
    zero = lax.shift_right_logical(lax.shift_right_logical(bits, 16), 16).astype(F32)
    return jnp.concatenate([zero[0:1, :]] * (width // LANES), axis=1)


def _residual_update(y_ref, x_ref, wout_ref, npost_ref, gate_ref, o_ref):
    yo = jnp.dot(y_ref[...], wout_ref[...], preferred_element_type=F32)
    ms = jnp.mean(yo * yo, axis=-1, keepdims=True)
    o_ref[...] = x_ref[...] + gate_ref[...] * (yo * lax.rsqrt(ms + NORM_EPS) * npost_ref[...])


def _layer_call(body, name, tile, x, scale1, shift, gate, norm_pre, w_in, norm_post, w_out, params, scratch):
    bsz, t, d = x.shape
    n_tiles = t // tile
    last = bsz * n_tiles - 1
    cols = w_in.shape[1]
    row = lambda a: a.reshape(1, -1)
    full = lambda a: pl.BlockSpec(a.shape, lambda s: (0,) * a.ndim)
    tile_in = lambda s: jnp.minimum(s, last)
    tile_out = lambda s: jnp.maximum(s - PIPE_LAG, 0)
    xn_spec = pl.BlockSpec((tile, d), lambda s: (tile_in(s), 0))
    xr_spec = pl.BlockSpec((tile, d), lambda s: (tile_out(s), 0))
    mod_in = pl.BlockSpec((None, 1, d), lambda s: (tile_in(s) // n_tiles, 0, 0))
    mod_out = pl.BlockSpec((None, 1, d), lambda s: (tile_out(s) // n_tiles, 0, 0))
    shared = [row(norm_pre), w_in, row(norm_post), w_out] + list(params)
    x2 = x.reshape(bsz * t, d)
    out = pl.pallas_call(
        functools.partial(body, n_tiles=n_tiles),
        grid=(bsz * n_tiles + PIPE_LAG,),
        in_specs=[xn_spec, xr_spec, mod_in, mod_in, mod_out] + [full(a) for a in shared],
        out_specs=xr_spec,
        out_shape=jax.ShapeDtypeStruct((bsz * t, d), F32),
        scratch_shapes=[
            pltpu.VMEM((tile, cols), F32),
            pltpu.VMEM((tile, 2 * WIDTH), BF16),
        ] + list(scratch),
        compiler_params=pltpu.CompilerParams(
            dimension_semantics=("arbitrary",), vmem_limit_bytes=VMEM_LIMIT),
        name=name,
    )(x2, x2, scale1, shift, gate, *shared)
    return out.reshape(bsz, t, d)


def _bd(x):
    lane = lax.broadcasted_iota(jnp.int32, x.shape, 1)
    return jnp.concatenate([jnp.where(lane < HEAD, x, 0.0), jnp.where(lane >= HEAD, x, 0.0)],
                           axis=0).astype(BF16)


def _bd2(z):
    return jnp.concatenate([_bd(z[:, :LANES]), _bd(z[:, LANES:])], axis=1)


def _tn_pair(x, y):
    full = _dot_tn(x, y)
    lane = lax.broadcasted_iota(jnp.int32, (HEAD, LANES), 1)
    return jnp.where(lane < HEAD, full[:HEAD], full[HEAD:])


def _seg_sum(x, ones_bd):
    rows = x.shape[0]
    n = x.shape[1] // LANES
    xs = jnp.concatenate([x[:, j * LANES:(j + 1) * LANES] for j in range(n)], axis=0)
    s = jnp.dot(xs.astype(BF16), ones_bd, preferred_element_type=F32)
    return jnp.concatenate([s[j * rows:(j + 1) * rows] for j in range(n)], axis=1)


def _even_layer_kernel(xn_ref, xr_ref, sc_ref, sh_ref, gate_ref, npre_ref, win_ref, npost_ref, wout_ref,
                       mu_ref, w0_ref, w2_ref, a0_ref, a2_ref, kk_ref, ka_ref, rk_ref, lng_ref, lnb_ref, cw_ref,
                       o_ref, pbuf_ref, y_ref, prow_ref, ht_ref, uc_ref, *, n_tiles):
    L = CHUNK
    T = xn_ref.shape[0]
    nsub = T // L
    W = WIDTH
    cat = jnp.concatenate
    step = pl.program_id(0)

    @pl.when(step == 0)
    def _():
        pbuf_ref[...] = jnp.zeros_like(pbuf_ref)
        y_ref[...] = jnp.zeros_like(y_ref)

    @pl.when((step == 0) | ((step + n_tiles - 1) % n_tiles == 0))
    def _():
        prow_ref[...] = jnp.zeros_like(prow_ref)
        ht_ref[...] = jnp.zeros_like(ht_ref)
        uc_ref[...] = jnp.zeros_like(uc_ref)

    _residual_update(y_ref, xr_ref, wout_ref, npost_ref, gate_ref, o_ref)

    p_ref = pbuf_ref
    p_tm = p_ref[:, :TM_COLS]
    o = TM_COLS
    b_gate = p_ref[:, o:o + W]
    c_gate = p_ref[:, o + W:o + 2 * W]
    hh = p_ref[:, o + 2 * W:o + 3 * W]
    g2 = p_ref[:, o + 3 * W:o + 4 * W]
    emit_proj = _projection_emitter(xn_ref, npre_ref, sc_ref, sh_ref, win_ref, pbuf_ref, block=4 * LANES)

    blk = emit_proj(2)

    u = c_gate * hh
    carry = uc_ref[...]
    conv = (cw_ref[0:1, :] * _shift_rows(u, carry, 2) + cw_ref[1:2, :] * _shift_rows(u, carry, 1) +
            cw_ref[2:3, :] * u)
    uc_ref[...] = u[T - 8:T, :]
    y_ref[:, W:] = (b_gate * conv * _silu(g2)).astype(BF16)

    prev = _shift_rows(p_tm, prow_ref[...], 1)
    prow_ref[...] = p_tm[T - 8:T, :]
    ps = p_tm + (prev - p_tm) * (mu_ref[...] + _zero_after(blk, TM_COLS))
    blk = emit_proj(2)
    r = ps[:, 0:W]
    k = ps[:, W:2 * W]
    v = ps[:, 2 * W:3 * W]
    g = ps[:, 3 * W:4 * W]
    wd = ps[:, 4 * W:4 * W + LORA]
    ad = ps[:, 4 * W + LORA:4 * W + 2 * LORA]
    w_log = -_softplus(-(w0_ref[...] + _zero_after(blk, W) + _dot(jnp.tanh(wd), w2_ref[...]))) - 0.5
    lw = -jnp.exp(w_log)
    blk = emit_proj(1)
    a_icl = _sigmoid(a0_ref[...] + _zero_after(blk, W) + _dot(ad, a2_ref[...]))
    k2 = k * (1.0 + (a_icl - 1.0) * ka_ref[...])
    ones_bd = ((lax.broadcasted_iota(jnp.int32, (LANES, LANES), 0) >> 6) ==
               (lax.broadcasted_iota(jnp.int32, (LANES, LANES), 1) >> 6)).astype(BF16)
    kk = k * kk_ref[...]
    kk = kk * lax.rsqrt(jnp.maximum(_seg_sum(kk * kk, ones_bd), 1e-24))
    kb = kk * a_icl
    blk = emit_proj(2)

    ti = lax.broadcasted_iota(jnp.int32, (T, T), 0)
    tj = lax.broadcasted_iota(jnp.int32, (T, T), 1)
    tri = ((ti >= tj) & ((ti >> 6) == (tj >> 6))).astype(F32)
    cum_incl = _dot_exact_lhs(tri, lw)
    cum_excl = cum_incl - lw
    rows_of = lambda x, i: x[i:i + 1, :]
    mid_rows = [rows_of(cum_incl, s * L + L // 2 - 1) for s in range(nsub)]
    end_rows = [rows_of(cum_incl, s * L + L - 1) for s in range(nsub)]
    mid = cat([jnp.broadcast_to(m, (L, W)) for m in mid_rows], axis=0)
    end = cat([jnp.broadcast_to(m, (L, W)) for m in end_rows], axis=0)
    g_mid = [jnp.exp(m) for m in mid_rows]
    g_end = [jnp.exp(m) for m in end_rows]
    e_k = jnp.exp(mid - cum_incl)
    e_end = jnp.exp(end - cum_incl)
    blk = emit_proj(2)
    at_all = -kk * jnp.exp(cum_excl - mid)
    rt_all = r * jnp.exp(cum_incl - mid)
    bt_all = kb * e_k
    kt_all = k2 * e_k
    bh_all = kb * e_end
    kh_all = k2 * e_end
    emit_proj()

    ri = lax.broadcasted_iota(jnp.int32, (L, LANES), 0)
    cj = lax.broadcasted_iota(jnp.int32, (L, LANES), 1) & (HEAD - 1)
    eye = (ri == cj).astype(F32)
    strict = ri > cj
    same4 = (ri >> 2) == (cj >> 2)
    same16 = (ri >> 4) == (cj >> 4)
    m0 = strict & same4
    m1 = strict & same16 & jnp.logical_not(same4)
    m2 = strict & jnp.logical_not(same16)
    r2 = lax.broadcasted_iota(jnp.int32, (2 * L, 2 * LANES), 0)
    c2 = lax.broadcasted_iota(jnp.int32, (2 * L, 2 * LANES), 1) & (HEAD - 1)
    mask_aa = (r2 & (L - 1)) >= jnp.where(r2 < L, c2 + 1, c2)
    zero_sq = jnp.zeros((LANES, LANES), BF16)

    cs = [(s, pr) for s in range(nsub) for pr in range(HEADS // 2)]
    tile = lambda x, c: x[c[0] * L:(c[0] + 1) * L, c[1] * LANES:(c[1] + 1) * LANES]
    at = [tile(at_all, c) for c in cs]
    rt = [tile(rt_all, c) for c in cs]
    vv = [tile(v, c) for c in cs]
    bh = [tile(bh_all, c) for c in cs]
    kh = [tile(kh_all, c) for c in cs]
    n = range(len(cs))

    aa = [_dot_nt(cat([at[i], rt[i]], axis=0), cat([_bd(tile(bt_all, c)), _bd(tile(kt_all, c))], axis=0))
          for i, c in enumerate(cs)]
    aa = [jnp.where(mask_aa, x, 0.0) for x in aa]
    n_ab = [x[:L, :LANES] for x in aa]
    akv = [_dot(aa[i][:L, LANES:], _bd(vv[i])) for i in n]

    n0 = [jnp.where(m0, x, 0.0) for x in n_ab]
    n0sq = [_dot(x, _bd(x)) for x in n0]
    d0 = [eye + n0[i] + n0sq[i] + _dot(n0[i], _bd(n0sq[i])) for i in n]
    p1 = [_dot(d0[i], _bd(jnp.where(m1, n_ab[i], 0.0))) for i in n]
    pm = [_dot(p1[i], cat([_bd(p1[i]), _bd(d0[i])], axis=1)) for i in n]
    e1 = [d0[i] + pm[i][:, LANES:] for i in n]
    d1 = [e1[i] + _dot(pm[i][:, :LANES], _bd(e1[i])) for i in n]
    dm = [_dot(d1[i], cat([_bd(jnp.where(m2, n_ab[i], 0.0)), _bd(at[i]), _bd(akv[i])], axis=1))
          for i in n]
    p2 = [x[:, :LANES] for x in dm]
    z = [x[:, LANES:] for x in dm]
    qm = [_dot(p2[i], cat([_bd(p2[i]), _bd2(z[i])], axis=1)) for i in n]
    z = [z[i] + qm[i][:, LANES:] for i in n]
    z = [z[i] + _dot(qm[i][:, :LANES], _bd2(z[i])) for i in n]

    top = [_dot(aa[i][L:, :], cat([_bd2(z[i]), cat([zero_sq, _bd(vv[i])], axis=1)], axis=0))
           for i in n]
    gm = [g_mid[c[0]][:, c[1] * LANES:(c[1] + 1) * LANES] for c in cs]
    ge = [g_end[c[0]][:, c[1] * LANES:(c[1] + 1) * LANES] for c in cs]
    q_eff = [(rt[i] + top[i][:, :LANES]) * gm[i] for i in n]
    m_low = [_tn_pair(bh[i], z[i][:, :LANES]) * gm[i] for i in n]
    g_t = [_tn_pair(cat([z[i][:, LANES:], vv[i]], axis=0), cat([bh[i], kh[i]], axis=0)) for i in n]

    state = [ht_ref[pr] for pr in range(HEADS // 2)]
    y_tiles = []
    for i, (s, pr) in enumerate(cs):
        y_tiles.append(_dot_nt(q_eff[i], _bd(state[pr])) + top[i][:, LANES:])
        state[pr] = state[pr] * ge[i] + _dot_nt(state[pr], _bd(m_low[i])) + g_t[i]
    for pr in range(HEADS // 2):
        ht_ref[pr] = state[pr]

    npr = HEADS // 2
    y = cat([cat(y_tiles[s * npr:(s + 1) * npr], axis=1) for s in range(nsub)], axis=0)
    inv_n = 1.0 / HEAD
    yc = y - _seg_sum(y, ones_bd) * inv_n
    emit_proj(1)
    var = _seg_sum(yc * yc, ones_bd) * inv_n
    y_n = yc * lax.rsqrt(var + TM_LN_EPS) * lng_ref[...] + lnb_ref[...]
    emit_proj(1)
    bonus = _seg_sum(r * k2 * rk_ref[...], ones_bd) * v
    y_ref[:, :W] = ((y_n + bonus) * _silu(g)).astype(BF16)
    emit_proj()


def _even_layer(x, scale1, shift, gate, norm_pre, w_in, norm_post, w_out, mu, w0, w2, a0, a2, k_k, k_a, r_k,
                lnx_g, lnx_b, conv_w):
    row = lambda a: a.reshape(1, -1)
    params = [row(mu), row(w0), w2, row(a0), a2, row(k_k), row(k_a), row(r_k), row(lnx_g), row(lnx_b), conv_w]
    scratch = [
        pltpu.VMEM((8, TM_COLS), F32),
        pltpu.VMEM((HEADS // 2, HEAD, LANES), F32),
        pltpu.VMEM((8, WIDTH), F32),
    ]
    return _layer_call(_even_layer_kernel, "even_layer", EVEN_TILE, x, scale1, shift, gate, norm_pre, w_in,
                       norm_post, w_out, params, scratch)


def _odd_layer_kernel(xn_ref, xr_ref, sc_ref, sh_ref, gate_ref, npre_ref, win_ref, npost_ref, wout_ref,
                      cfw_ref, cfb_ref, cfg_ref, cfbb_ref, scw_ref, scb_ref, dtb_ref, alog_ref, dsk_ref, ng_ref,
                      o_ref, pbuf_ref, y_ref, ubuf_ref, xc_ref, st_ref, *, n_tiles):
    L = CHUNK
    T = xn_ref.shape[0]
    nsub = T // L
    W = WIDTH
    cat = jnp.concatenate
    step = pl.program_id(0)

    @pl.when(step == 0)
    def _():
        pbuf_ref[...] = jnp.zeros_like(pbuf_ref)
        y_ref[...] = jnp.zeros_like(y_ref)

    @pl.when((step == 0) | ((step + n_tiles - 1) % n_tiles == 0))
    def _():
        ubuf_ref[...] = jnp.zeros_like(ubuf_ref)
        xc_ref[...] = jnp.zeros_like(xc_ref)
        st_ref[...] = jnp.zeros_like(st_ref)

    _residual_update(y_ref, xr_ref, wout_ref, npost_ref, gate_ref, o_ref)

    p_ref = pbuf_ref
    val = p_ref[:, 0:W]
    glu = p_ref[:, W:2 * W]
    g = p_ref[:, 2 * W:3 * W]
    o = 3 * W
    z = p_ref[:, o:o + W]
    xbc_in = p_ref[:, o + W:o + W + SSD_XBC]
    dt_in = p_ref[:, o + W + SSD_XBC:o + W + SSD_XBC + LANES]
    emit_proj = _projection_emitter(xn_ref, npre_ref, sc_ref, sh_ref, win_ref, pbuf_ref, block=4 * LANES)

    ubuf_ref[CF_HALO:CF_HALO + T, :] = val * _sigmoid(glu)
    first = CF_HALO - (CF_KERNEL - 1)
    acc = jnp.zeros((T, W), F32) + cfb_ref[...]
    proj_blk = None
    for rr in range(8):
        part = None
        for i in range(CF_KERNEL):
            off = first + i
            if off % 8 != rr:
                continue
            w_row = cfw_ref[i:i + 1, :]
            if part is None and proj_blk is not None:
                w_row = w_row + _zero_after(proj_blk, W)
            term = w_row * ubuf_ref[off - rr:off - rr + T + 8, :]
            part = term if part is None else part + term
        if rr == 0:
            acc = acc + part[:T]
        else:
            acc = acc + pltpu.roll(part, T + 8 - rr, 0)[:T]
        proj_blk = emit_proj(1)
    ubuf_ref[:CF_HALO, :] = ubuf_ref[T:T + CF_HALO, :]
    mean = jnp.mean(acc, axis=-1, keepdims=True)
    ac = acc - mean
    var = jnp.mean(ac * ac, axis=-1, keepdims=True)
    ln = ac * lax.rsqrt(var + CF_LN_EPS) * cfg_ref[...] + cfbb_ref[...]
    y_ref[:, :W] = (_silu(ln) * _silu(g)).astype(BF16)
    emit_proj(1)

    carry = xc_ref[...]
    conv = (scw_ref[0:1, :] * _shift_rows(xbc_in, carry, 3) + scw_ref[1:2, :] * _shift_rows(xbc_in, carry, 2) +
            scw_ref[2:3, :] * _shift_rows(xbc_in, carry, 1) + scw_ref[3:4, :] * xbc_in + scb_ref[...])
    xc_ref[...] = xbc_in[T - 8:T, :]
    emit_proj(1)
    xbc = _silu(conv)
    emit_proj(1)
    xs = xbc[:, :W]
    dt = _softplus(dt_in + dtb_ref[...])
    da = dt * (-jnp.exp(alog_ref[...]))
    ti = lax.broadcasted_iota(jnp.int32, (T, T), 0)
    tj = lax.broadcasted_iota(jnp.int32, (T, T), 1)
    tri = ((ti >= tj) & ((ti >> 6) == (tj >> 6))).astype(F32)
    cs = _dot_exact_lhs(tri, da)
    expand = ((lax.broadcasted_iota(jnp.int32, (LANES, W), 1) >> 6) ==
              lax.broadcasted_iota(jnp.int32, (LANES, W), 0)).astype(BF16)

    def widen(x):
        hi, lo = _split_hi_lo(x)
        return (jnp.dot(hi, expand, preferred_element_type=F32) + jnp.dot(lo, expand, preferred_element_type=F32))

    cs_w = widen(cs)
    x_dt = xs * widen(dt)
    ri = lax.broadcasted_iota(jnp.int32, (L, W), 0)
    cj = lax.broadcasted_iota(jnp.int32, (L, W), 1) & (HEAD - 1)
    diag = (ri == cj).astype(F32)
    causal = ri >= cj
    ones_ll = jnp.ones((L, L), F32)

    y_chunks = []
    state = [st_ref[pr] for pr in range(HEADS // 2)]
    for s in range(nsub):
        rows = slice(s * L, (s + 1) * L)
        csw = cs_w[rows]
        last = csw[L - 1:L, :]
        row_b = _dot_exact_lhs(ones_ll, csw * diag)
        lmat = jnp.exp(jnp.where(causal, csw - row_b, -1e30))
        xd = x_dt[rows]
        xd_dec = xd * jnp.exp(last - csw)
        e_cs = jnp.exp(csw)
        e_last = jnp.exp(last)
        tiles = []
        for grp in range(SSD_GROUPS):
            bm = xbc[rows, W + grp * SSD_STATE:W + (grp + 1) * SSD_STATE]
            cm = xbc[rows, W + (SSD_GROUPS + grp) * SSD_STATE:W + (SSD_GROUPS + grp + 1) * SSD_STATE]
            cb2 = _dot_nt(cm, cat([bm, bm], axis=0))
            for e in range(SSD_HPG // 2):
                pr = grp * (SSD_HPG // 2) + e
                lanes = slice(pr * LANES, (pr + 1) * LANES)
                y_diag = _dot(cb2 * lmat[:, lanes], _bd(xd[:, lanes]))
                y_off = _dot(cm, state[pr]) * e_cs[:, lanes]
                state[pr] = state[pr] * e_last[:, lanes] + _dot_tn(bm, xd_dec[:, lanes])
                tiles.append(y_diag + y_off)
        y_chunks.append(cat(tiles, axis=1))
    for pr in range(HEADS // 2):
        st_ref[pr] = state[pr]
    y = cat(y_chunks, axis=0) + xs * dsk_ref[...]
    y = y * _silu(z)
    ms = jnp.mean(y * y, axis=-1, keepdims=True)
    y_ref[:, W:] = (y * lax.rsqrt(ms + NORM_EPS) * ng_ref[...]).astype(BF16)
    emit_proj()


def _odd_layer(x, scale1, shift, gate, norm_pre, w_in, norm_post, w_out, cf_w, cf_b, cf_g, cf_bb, sc_w, sc_b,
               dt_bias, a_log, d_skip, norm_g):
    row = lambda a: a.reshape(1, -1)
    pad_lanes = lambda a: jnp.zeros((1, LANES), F32).at[0, :a.shape[0]].set(a)
    params = [cf_w, row(cf_b), row(cf_g), row(cf_bb), sc_w, row(sc_b), pad_lanes(dt_bias), pad_lanes(a_log),
              row(jnp.repeat(d_skip, HEAD)), row(norm_g)]
    scratch = [
        pltpu.VMEM((CF_HALO + ODD_TILE + 8, WIDTH), F32),
        pltpu.VMEM((8, SSD_XBC), F32),
        pltpu.VMEM((HEADS // 2, SSD_STATE, LANES), F32),
    ]
    return _layer_call(_odd_layer_kernel, "odd_layer", ODD_TILE, x, scale1, shift, gate, norm_pre, w_in,
                       norm_post, w_out, params, scratch)


def kernel(x, c, ada_w, ada_b, norm_pre, norm_post, ev_w_in, ev_w_out, tm_mu, tm_w0, tm_w2, tm_a0, tm_a2,
           tm_k_k, tm_k_a, tm_r_k, tm_lnx_g, tm_lnx_b, sc_conv_w, od_w_in, od_w_out, cf_conv_w, cf_conv_b,
           cf_ln_g, cf_ln_b, ssd_conv_w, ssd_conv_b, ssd_dt_bias, ssd_a_log, ssd_d, ssd_norm_g):
    depth = ada_w.shape[0]
    bsz, t, d = x.shape
    assert d == D_MODEL and t % EVEN_TILE == 0 and t % ODD_TILE == 0
    mod = _ada_modulation(c, ada_w, ada_b)
    od_cols = od_w_in.shape[-1]
    od_w_pad = jnp.zeros(od_w_in.shape[:2] + (ODD_COLS_PAD,), BF16).at[..., :od_cols].set(od_w_in.astype(BF16))
    for i in range(depth):
        shift = mod[i, :, None, 0:d]
        scale1 = 1.0 + mod[i, :, None, d:2 * d]
        gate = mod[i, :, None, 2 * d:3 * d]
        j = i // 2
        if i % 2 == 0:
            x = _even_layer(x, scale1, shift, gate, norm_pre[i], ev_w_in[j].astype(BF16), norm_post[i],
                            ev_w_out[j].astype(BF16), tm_mu[j], tm_w0[j], tm_w2[j], tm_a0[j], tm_a2[j],
                            tm_k_k[j].reshape(-1), tm_k_a[j].reshape(-1), tm_r_k[j].reshape(-1), tm_lnx_g[j],
                            tm_lnx_b[j], sc_conv_w[j])
        else:
            x = _odd_layer(x, scale1, shift, gate, norm_pre[i], od_w_pad[j], norm_post[i],
                           od_w_out[j].astype(BF16), cf_conv_w[j], cf_conv_b[j], cf_ln_g[j], cf_ln_b[j],
                           ssd_conv_w[j], ssd_conv_b[j], ssd_dt_bias[j], ssd_a_log[j], ssd_d[j], ssd_norm_g[j])
    return x
```

```python
import functools

import jax
import jax.numpy as jnp
from jax import lax
from jax.experimental import pallas as pl
from jax.experimental.pallas import tpu as pltpu

F32 = jnp.float32
BF16 = jnp.bfloat16

D_MODEL = 1024
CHUNK = 64
EVEN_TILE = 512
ODD_TILE = 512
HEAD = 64
HEADS = 8
WIDTH = 512
LORA = 64
TM_COLS = 4 * WIDTH + 2 * LORA
EVEN_COLS = TM_COLS + 4 * WIDTH
SSD_STATE = 128
SSD_GROUPS = 2
SSD_HPG = HEADS // SSD_GROUPS
SSD_XBC = WIDTH + 2 * SSD_GROUPS * SSD_STATE
CF_KERNEL = 31
CF_HALO = 32
LANES = 128
ODD_COLS_PAD = 3 * WIDTH + WIDTH + SSD_XBC + LANES
NORM_EPS = 1e-6
TM_LN_EPS = 64e-5
CF_LN_EPS = 1e-5
VMEM_LIMIT = 56 * 1024 * 1024


def _dot(a, b):
    return jnp.dot(a.astype(BF16), b.astype(BF16), preferred_element_type=F32)


def _dot_nt(a, b):
    return lax.dot_general(a.astype(BF16), b.astype(BF16), (((1,), (1,)), ((), ())),
                           preferred_element_type=F32)


def _dot_tn(a, b):
    return lax.dot_general(a.astype(BF16), b.astype(BF16), (((0,), (0,)), ((), ())),
                           preferred_element_type=F32)


def _split_hi_lo(x):
    hi = x.astype(BF16)
    lo = (x - hi.astype(F32)).astype(BF16)
    return hi, lo


def _dot_exact_lhs(m, x):
    hi, lo = _split_hi_lo(x)
    mb = m.astype(BF16)
    return (jnp.dot(mb, hi, preferred_element_type=F32) +
            jnp.dot(mb, lo, preferred_element_type=F32))


def _sigmoid(x):
    return 0.5 * jnp.tanh(0.5 * x) + 0.5


def _silu(x):
    h = 0.5 * x
    return h + h * jnp.tanh(h)


def _softplus(x):
    return jnp.maximum(x, 0.0) + jnp.log(1.0 + jnp.exp(-jnp.abs(x)))


def _shift_rows(x, carry, d):
    out = pltpu.roll(x, d, 0)
    row = lax.broadcasted_iota(jnp.int32, (8, 1), 0)
    head = jnp.where(row < d, pltpu.roll(carry, d, 0), out[:8])
    return jnp.concatenate([head, out[8:]], axis=0)


def _ada_kernel(c_ref, w_ref, b_ref, o_ref):
    ca = _silu(c_ref[...])
    o_ref[...] = jnp.dot(ca, w_ref[...], preferred_element_type=F32,
                         precision=lax.Precision.HIGHEST) + b_ref[...]


def _ada_modulation(c, ada_w, ada_b):
    depth, d, d3 = ada_w.shape
    bsz = c.shape[0]
    rows = 8
    c_pad = jnp.zeros((rows, d), F32).at[:bsz].set(c)
    tn = 768
    out = pl.pallas_call(
        _ada_kernel,
        grid=(depth, d3 // tn),
        in_specs=[
            pl.BlockSpec((rows, d), lambda i, j: (0, 0)),
            pl.BlockSpec((None, d, tn), lambda i, j: (i, 0, j)),
            pl.BlockSpec((None, 1, tn), lambda i, j: (i, 0, j)),
        ],
        out_specs=pl.BlockSpec((None, rows, tn), lambda i, j: (i, 0, j)),
        out_shape=jax.ShapeDtypeStruct((depth, rows, d3), F32),
        compiler_params=pltpu.CompilerParams(
            dimension_semantics=("parallel", "parallel"), vmem_limit_bytes=VMEM_LIMIT),
        name="ada_modulation",
    )(c_pad, ada_w, ada_b.reshape(depth, 1, d3))
    return out[:, :bsz]


PIPE_LAG = 2


PROJ_BLOCK = 3 * LANES


def _projection_emitter(xn_ref, npre_ref, sc_ref, sh_ref, win_ref, p_ref, block=PROJ_BLOCK):
    x = xn_ref[...]
    ms = jnp.mean(x * x, axis=-1, keepdims=True)
    h = x * lax.rsqrt(ms + NORM_EPS) * npre_ref[...]
    h = (h * sc_ref[...] + sh_ref[...]).astype(BF16)
    cols = win_ref.shape[1]
    starts = list(range(0, cols, block))

    def emit(k=None):
        blk = None
        for _ in range(len(starts) if k is None else min(k, len(starts))):
            lo = starts.pop(0)
            hi = min(lo + block, cols)
            blk = jnp.dot(h, win_ref[:, lo:hi], preferred_element_type=F32)
            p_ref[:, lo:hi] = blk
        return blk

    return emit


def _zero_after(dep, width):
    bits = lax.bitcast_convert_type(dep[0:8, 0:LANES], jnp.int32)
    zero = lax.shift_right_logical(lax.shift_right_logical(bits, 16), 16).astype(F32)
    return jnp.concatenate([zero[0:1, :]] * (width // LANES), axis=1)


def _residual_update(y_ref, x_ref, wout_ref, npost_ref, gate_ref, o_ref):
    yo = jnp.dot(y_ref[...], wout_ref[...], preferred_element_type=F32)
    ms = jnp.mean(yo * yo, axis=-1, keepdims=True)
    o_ref[...] = x_ref[...] + gate_ref[...] * (yo * lax.rsqrt(ms + NORM_EPS) * npost_ref[...])


def _layer_call(body, name, tile, x, scale1, shift, gate, norm_pre, w_in, norm_post, w_out, params, scratch):
    bsz, t, d = x.shape
    n_tiles = t // tile
    last = bsz * n_tiles - 1
    cols = w_in.shape[1]
    row = lambda a: a.reshape(1, -1)
    full = lambda a: pl.BlockSpec(a.shape, lambda s: (0,) * a.ndim)
    tile_in = lambda s: jnp.minimum(s, last)
    tile_out = lambda s: jnp.maximum(s - PIPE_LAG, 0)
    xn_spec = pl.BlockSpec((tile, d), lambda s: (tile_in(s), 0))
    xr_spec = pl.BlockSpec((tile, d), lambda s: (tile_out(s), 0))
    mod_in = pl.BlockSpec((None, 1, d), lambda s: (tile_in(s) // n_tiles, 0, 0))
    mod_out = pl.BlockSpec((None, 1, d), lambda s: (tile_out(s) // n_tiles, 0, 0))
    shared = [row(norm_pre), w_in, row(norm_post), w_out] + list(params)
    x2 = x.reshape(bsz * t, d)
    out = pl.pallas_call(
        functools.partial(body, n_tiles=n_tiles),
        grid=(bsz * n_tiles + PIPE_LAG,),
        in_specs=[xn_spec, xr_spec, mod_in, mod_in, mod_out] + [full(a) for a in shared],
        out_specs=xr_spec,
        out_shape=jax.ShapeDtypeStruct((bsz * t, d), F32),
        scratch_shapes=[
            pltpu.VMEM((tile, cols), F32),
            pltpu.VMEM((tile, 2 * WIDTH), BF16),
        ] + list(scratch),
        compiler_params=pltpu.CompilerParams(
            dimension_semantics=("arbitrary",), vmem_limit_bytes=VMEM_LIMIT),
        name=name,
    )(x2, x2, scale1, shift, gate, *shared)
    return out.reshape(bsz, t, d)


def _bd(x):
    lane = lax.broadcasted_iota(jnp.int32, x.shape, 1)
    return jnp.concatenate([jnp.where(lane < HEAD, x, 0.0), jnp.where(lane >= HEAD, x, 0.0)],
                           axis=0).astype(BF16)


def _bd2(z):
    return jnp.concatenate([_bd(z[:, :LANES]), _bd(z[:, LANES:])], axis=1)


def _tn_pair(x, y):
    full = _dot_tn(x, y)
    lane = lax.broadcasted_iota(jnp.int32, (HEAD, LANES), 1)
    return jnp.where(lane < HEAD, full[:HEAD], full[HEAD:])


def _seg_sum(x, ones_bd):
    rows = x.shape[0]
    n = x.shape[1] // LANES
    xs = jnp.concatenate([x[:, j * LANES:(j + 1) * LANES] for j in range(n)], axis=0)
    s = jnp.dot(xs.astype(BF16), ones_bd, preferred_element_type=F32)
    return jnp.concatenate([s[j * rows:(j + 1) * rows] for j in range(n)], axis=1)


def _even_layer_kernel(xn_ref, xr_ref, sc_ref, sh_ref, gate_ref, npre_ref, win_ref, npost_ref, wout_ref,
                       mu_ref, w0_ref, w2_ref, a0_ref, a2_ref, kk_ref, ka_ref, rk_ref, lng_ref, lnb_ref, cw_ref,
                       o_ref, pbuf_ref, y_ref, prow_ref, ht_ref, uc_ref, *, n_tiles):
    L = CHUNK
    T = xn_ref.shape[0]
    nsub = T // L
    W = WIDTH
    cat = jnp.concatenate
    step = pl.program_id(0)

    @pl.when(step == 0)
    def _():
        pbuf_ref[...] = jnp.zeros_like(pbuf_ref)
        y_ref[...] = jnp.zeros_like(y_ref)

    @pl.when((step == 0) | ((step + n_tiles - 1) % n_tiles == 0))
    def _():
        prow_ref[...] = jnp.zeros_like(prow_ref)
        ht_ref[...] = jnp.zeros_like(ht_ref)
        uc_ref[...] = jnp.zeros_like(uc_ref)

    _residual_update(y_ref, xr_ref, wout_ref, npost_ref, gate_ref, o_ref)

    p_ref = pbuf_ref
    p_tm = p_ref[:, :TM_COLS]
    o = TM_COLS
    b_gate = p_ref[:, o:o + W]
    c_gate = p_ref[:, o + W:o + 2 * W]
    hh = p_ref[:, o + 2 * W:o + 3 * W]
    g2 = p_ref[:, o + 3 * W:o + 4 * W]
    emit_proj = _projection_emitter(xn_ref, npre_ref, sc_ref, sh_ref, win_ref, pbuf_ref, block=4 * LANES)

    blk = emit_proj(1)

    u = c_gate * hh
    carry = uc_ref[...]
    conv = (cw_ref[0:1, :] * _shift_rows(u, carry, 2) + cw_ref[1:2, :] * _shift_rows(u, carry, 1) +
            cw_ref[2:3, :] * u)
    uc_ref[...] = u[T - 8:T, :]
    y_ref[:, W:] = (b_gate * conv * _silu(g2)).astype(BF16)

    prev = _shift_rows(p_tm, prow_ref[...], 1)
    prow_ref[...] = p_tm[T - 8:T, :]
    ps = p_tm + (prev - p_tm) * (mu_ref[...] + _zero_after(blk, TM_COLS))
    blk = emit_proj(1)
    r = ps[:, 0:W]
    k = ps[:, W:2 * W]
    v = ps[:, 2 * W:3 * W]
    g = ps[:, 3 * W:4 * W]
    wd = ps[:, 4 * W:4 * W + LORA]
    ad = ps[:, 4 * W + LORA:4 * W + 2 * LORA]
    w_log = -_softplus(-(w0_ref[...] + _zero_after(blk, W) + _dot(jnp.tanh(wd), w2_ref[...]))) - 0.5
    lw = -jnp.exp(w_log)
    blk = emit_proj(1)
    a_icl = _sigmoid(a0_ref[...] + _zero_after(blk, W) + _dot(ad, a2_ref[...]))
    k2 = k * (1.0 + (a_icl - 1.0) * ka_ref[...])
    ones_bd = ((lax.broadcasted_iota(jnp.int32, (LANES, LANES), 0) >> 6) ==
               (lax.broadcasted_iota(jnp.int32, (LANES, LANES), 1) >> 6)).astype(BF16)
    kk = k * kk_ref[...]
    kk = kk * lax.rsqrt(jnp.maximum(_seg_sum(kk * kk, ones_bd), 1e-24))
    kb = kk * a_icl
    blk = emit_proj(2)

    ti = lax.broadcasted_iota(jnp.int32, (T, T), 0)
    tj = lax.broadcasted_iota(jnp.int32, (T, T), 1)
    tri = ((ti >= tj) & ((ti >> 6) == (tj >> 6))).astype(F32)
    cum_incl = _dot_exact_lhs(tri, lw)
    cum_excl = cum_incl - lw
    rows_of = lambda x, i: x[i:i + 1, :]
    mid_rows = [rows_of(cum_incl, s * L + L // 2 - 1) for s in range(nsub)]
    end_rows = [rows_of(cum_incl, s * L + L - 1) for s in range(nsub)]
    mid = cat([jnp.broadcast_to(m, (L, W)) for m in mid_rows], axis=0)
    end = cat([jnp.broadcast_to(m, (L, W)) for m in end_rows], axis=0)
    g_mid = [jnp.exp(m) for m in mid_rows]
    g_end = [jnp.exp(m) for m in end_rows]
    e_k = jnp.exp(mid - cum_incl)
    e_end = jnp.exp(end - cum_incl)
    blk = emit_proj(2)
    at_all = -kk * jnp.exp(cum_excl - mid)
    rt_all = r * jnp.exp(cum_incl - mid)
    bt_all = kb * e_k
    kt_all = k2 * e_k
    bh_all = kb * e_end
    kh_all = k2 * e_end
    emit_proj()

    ri = lax.broadcasted_iota(jnp.int32, (L, LANES), 0)
    cj = lax.broadcasted_iota(jnp.int32, (L, LANES), 1) & (HEAD - 1)
    eye = (ri == cj).astype(F32)
    strict = ri > cj
    same4 = (ri >> 2) == (cj >> 2)
    same16 = (ri >> 4) == (cj >> 4)
    m0 = strict & same4
    m1 = strict & same16 & jnp.logical_not(same4)
    m2 = strict & jnp.logical_not(same16)
    r2 = lax.broadcasted_iota(jnp.int32, (2 * L, 2 * LANES), 0)
    c2 = lax.broadcasted_iota(jnp.int32, (2 * L, 2 * LANES), 1) & (HEAD - 1)
    mask_aa = (r2 & (L - 1)) >= jnp.where(r2 < L, c2 + 1, c2)
    zero_sq = jnp.zeros((LANES, LANES), BF16)

    cs = [(s, pr) for s in range(nsub) for pr in range(HEADS // 2)]
    tile = lambda x, c: x[c[0] * L:(c[0] + 1) * L, c[1] * LANES:(c[1] + 1) * LANES]
    at = [tile(at_all, c) for c in cs]
    rt = [tile(rt_all, c) for c in cs]
    vv = [tile(v, c) for c in cs]
    bh = [tile(bh_all, c) for c in cs]
    kh = [tile(kh_all, c) for c in cs]
    n = range(len(cs))

    aa = [_dot_nt(cat([at[i], rt[i]], axis=0), cat([_bd(tile(bt_all, c)), _bd(tile(kt_all, c))], axis=0))
          for i, c in enumerate(cs)]
    aa = [jnp.where(mask_aa, x, 0.0) for x in aa]
    n_ab = [x[:L, :LANES] for x in aa]
    akv = [_dot(aa[i][:L, LANES:], _bd(vv[i])) for i in n]

    n0 = [jnp.where(m0, x, 0.0) for x in n_ab]
    n0sq = [_dot(x, _bd(x)) for x in n0]
    d0 = [eye + n0[i] + n0sq[i] + _dot(n0[i], _bd(n0sq[i])) for i in n]
    p1 = [_dot(d0[i], _bd(jnp.where(m1, n_ab[i], 0.0))) for i in n]
    pm = [_dot(p1[i], cat([_bd(p1[i]), _bd(d0[i])], axis=1)) for i in n]
    e1 = [d0[i] + pm[i][:, LANES:] for i in n]
    d1 = [e1[i] + _dot(pm[i][:, :LANES], _bd(e1[i])) for i in n]
    dm = [_dot(d1[i], cat([_bd(jnp.where(m2, n_ab[i], 0.0)), _bd(at[i]), _bd(akv[i])], axis=1))
          for i in n]
    p2 = [x[:, :LANES] for x in dm]
    z = [x[:, LANES:] for x in dm]
    qm = [_dot(p2[i], cat([_bd(p2[i]), _bd2(z[i])], axis=1)) for i in n]
    z = [z[i] + qm[i][:, LANES:] for i in n]
    z = [z[i] + _dot(qm[i][:, :LANES], _bd2(z[i])) for i in n]

    top = [_dot(aa[i][L:, :], cat([_bd2(z[i]), cat([zero_sq, _bd(vv[i])], axis=1)], axis=0))
           for i in n]
    gm = [g_mid[c[0]][:, c[1] * LANES:(c[1] + 1) * LANES] for c in cs]
    ge = [g_end[c[0]][:, c[1] * LANES:(c[1] + 1) * LANES] for c in cs]
    q_eff = [(rt[i] + top[i][:, :LANES]) * gm[i] for i in n]
    m_low = [_tn_pair(bh[i], z[i][:, :LANES]) * gm[i] for i in n]
    g_t = [_tn_pair(cat([z[i][:, LANES:], vv[i]], axis=0), cat([bh[i], kh[i]], axis=0)) for i in n]

    state = [ht_ref[pr] for pr in range(HEADS // 2)]
    y_tiles = []
    for i, (s, pr) in enumerate(cs):
        y_tiles.append(_dot_nt(q_eff[i], _bd(state[pr])) + top[i][:, LANES:])
        state[pr] = state[pr] * ge[i] + _dot_nt(state[pr], _bd(m_low[i])) + g_t[i]
    for pr in range(HEADS // 2):
        ht_ref[pr] = state[pr]

    npr = HEADS // 2
    y = cat([cat(y_tiles[s * npr:(s + 1) * npr], axis=1) for s in range(nsub)], axis=0)
    inv_n = 1.0 / HEAD
    yc = y - _seg_sum(y, ones_bd) * inv_n
    emit_proj(1)
    var = _seg_sum(yc * yc, ones_bd) * inv_n
    y_n = yc * lax.rsqrt(var + TM_LN_EPS) * lng_ref[...] + lnb_ref[...]
    emit_proj(1)
    bonus = _seg_sum(r * k2 * rk_ref[...], ones_bd) * v
    y_ref[:, :W] = ((y_n + bonus) * _silu(g)).astype(BF16)
    emit_proj()


def _even_layer(x, scale1, shift, gate, norm_pre, w_in, norm_post, w_out, mu, w0, w2, a0, a2, k_k, k_a, r_k,
                lnx_g, lnx_b, conv_w):
    row = lambda a: a.reshape(1, -1)
    params = [row(mu), row(w0), w2, row(a0), a2, row(k_k), row(k_a), row(r_k), row(lnx_g), row(lnx_b), conv_w]
    scratch = [
        pltpu.VMEM((8, TM_COLS), F32),
        pltpu.VMEM((HEADS // 2, HEAD, LANES), F32),
        pltpu.VMEM((8, WIDTH), F32),
    ]
    return _layer_call(_even_layer_kernel, "even_layer", EVEN_TILE, x, scale1, shift, gate, norm_pre, w_in,
                       norm_post, w_out, params, scratch)


def _odd_layer_kernel(xn_ref, xr_ref, sc_ref, sh_ref, gate_ref, npre_ref, win_ref, npost_ref, wout_ref,
                      cfw_ref, cfb_ref, cfg_ref, cfbb_ref, scw_ref, scb_ref, dtb_ref, alog_ref, dsk_ref, ng_ref,
                      o_ref, pbuf_ref, y_ref, ubuf_ref, xc_ref, st_ref, *, n_tiles):
    L = CHUNK
    T = xn_ref.shape[0]
    nsub = T // L
    W = WIDTH
    cat = jnp.concatenate
    step = pl.program_id(0)

    @pl.when(step == 0)
    def _():
        pbuf_ref[...] = jnp.zeros_like(pbuf_ref)
        y_ref[...] = jnp.zeros_like(y_ref)

    @pl.when((step == 0) | ((step + n_tiles - 1) % n_tiles == 0))
    def _():
        ubuf_ref[...] = jnp.zeros_like(ubuf_ref)
        xc_ref[...] = jnp.zeros_like(xc_ref)
        st_ref[...] = jnp.zeros_like(st_ref)

    _residual_update(y_ref, xr_ref, wout_ref, npost_ref, gate_ref, o_ref)

    p_ref = pbuf_ref
    val = p_ref[:, 0:W]
    glu = p_ref[:, W:2 * W]
    g = p_ref[:, 2 * W:3 * W]
    o = 3 * W
    z = p_ref[:, o:o + W]
    xbc_in = p_ref[:, o + W:o + W + SSD_XBC]
    dt_in = p_ref[:, o + W + SSD_XBC:o + W + SSD_XBC + LANES]
    emit_proj = _projection_emitter(xn_ref, npre_ref, sc_ref, sh_ref, win_ref, pbuf_ref, block=4 * LANES)

    ubuf_ref[CF_HALO:CF_HALO + T, :] = val * _sigmoid(glu)
    first = CF_HALO - (CF_KERNEL - 1)
    acc = jnp.zeros((T, W), F32) + cfb_ref[...]
    proj_blk = None
    for rr in range(8):
        part = None
        for i in range(CF_KERNEL):
            off = first + i
            if off % 8 != rr:
                continue
            w_row = cfw_ref[i:i + 1, :]
            if part is None and proj_blk is not None:
                w_row = w_row + _zero_after(proj_blk, W)
            term = w_row * ubuf_ref[off - rr:off - rr + T + 8, :]
            part = term if part is None else part + term
        if rr == 0:
            acc = acc + part[:T]
        else:
            acc = acc + pltpu.roll(part, T + 8 - rr, 0)[:T]
        proj_blk = emit_proj(1)
    ubuf_ref[:CF_HALO, :] = ubuf_ref[T:T + CF_HALO, :]
    mean = jnp.mean(acc, axis=-1, keepdims=True)
    ac = acc - mean
    var = jnp.mean(ac * ac, axis=-1, keepdims=True)
    ln = ac * lax.rsqrt(var + CF_LN_EPS) * cfg_ref[...] + cfbb_ref[...]
    y_ref[:, :W] = (_silu(ln) * _silu(g)).astype(BF16)
    emit_proj(1)

    carry = xc_ref[...]
    conv = (scw_ref[0:1, :] * _shift_rows(xbc_in, carry, 3) + scw_ref[1:2, :] * _shift_rows(xbc_in, carry, 2) +
            scw_ref[2:3, :] * _shift_rows(xbc_in, carry, 1) + scw_ref[3:4, :] * xbc_in + scb_ref[...])
    xc_ref[...] = xbc_in[T - 8:T, :]
    emit_proj(1)
    xbc = _silu(conv)
    emit_proj(1)
    xs = xbc[:, :W]
    dt = _softplus(dt_in + dtb_ref[...])
    da = dt * (-jnp.exp(alog_ref[...]))
    ti = lax.broadcasted_iota(jnp.int32, (T, T), 0)
    tj = lax.broadcasted_iota(jnp.int32, (T, T), 1)
    tri = ((ti >= tj) & ((ti >> 6) == (tj >> 6))).astype(F32)
    cs = _dot_exact_lhs(tri, da)
    expand = ((lax.broadcasted_iota(jnp.int32, (LANES, W), 1) >> 6) ==
              lax.broadcasted_iota(jnp.int32, (LANES, W), 0)).astype(BF16)

    def widen(x):
        hi, lo = _split_hi_lo(x)
        return (jnp.dot(hi, expand, preferred_element_type=F32) + jnp.dot(lo, expand, preferred_element_type=F32))

    cs_w = widen(cs)
    x_dt = xs * widen(dt)
    ri = lax.broadcasted_iota(jnp.int32, (L, W), 0)
    cj = lax.broadcasted_iota(jnp.int32, (L, W), 1) & (HEAD - 1)
    diag = (ri == cj).astype(F32)
    causal = ri >= cj
    ones_ll = jnp.ones((L, L), F32)

    y_chunks = []
    state = [st_ref[pr] for pr in range(HEADS // 2)]
    for s in range(nsub):
        rows = slice(s * L, (s + 1) * L)
        csw = cs_w[rows]
        last = csw[L - 1:L, :]
        row_b = _dot_exact_lhs(ones_ll, csw * diag)
        lmat = jnp.exp(jnp.where(causal, csw - row_b, -1e30))
        xd = x_dt[rows]
        xd_dec = xd * jnp.exp(last - csw)
        e_cs = jnp.exp(csw)
        e_last = jnp.exp(last)
        tiles = []
        for grp in range(SSD_GROUPS):
            bm = xbc[rows, W + grp * SSD_STATE:W + (grp + 1) * SSD_STATE]
            cm = xbc[rows, W + (SSD_GROUPS + grp) * SSD_STATE:W + (SSD_GROUPS + grp + 1) * SSD_STATE]
            cb2 = _dot_nt(cm, cat([bm, bm], axis=0))
            for e in range(SSD_HPG // 2):
                pr = grp * (SSD_HPG // 2) + e
                lanes = slice(pr * LANES, (pr + 1) * LANES)
                y_diag = _dot(cb2 * lmat[:, lanes], _bd(xd[:, lanes]))
                y_off = _dot(cm, state[pr]) * e_cs[:, lanes]
                state[pr] = state[pr] * e_last[:, lanes] + _dot_tn(bm, xd_dec[:, lanes])
                tiles.append(y_diag + y_off)
        y_chunks.append(cat(tiles, axis=1))
    for pr in range(HEADS // 2):
        st_ref[pr] = state[pr]
    y = cat(y_chunks, axis=0) + xs * dsk_ref[...]
    y = y * _silu(z)
    ms = jnp.mean(y * y, axis=-1, keepdims=True)
    y_ref[:, W:] = (y * lax.rsqrt(ms + NORM_EPS) * ng_ref[...]).astype(BF16)
    emit_proj()


def _odd_layer(x, scale1, shift, gate, norm_pre, w_in, norm_post, w_out, cf_w, cf_b, cf_g, cf_bb, sc_w, sc_b,
               dt_bias, a_log, d_skip, norm_g):
    row = lambda a: a.reshape(1, -1)
    pad_lanes = lambda a: jnp.zeros((1, LANES), F32).at[0, :a.shape[0]].set(a)
    params = [cf_w, row(cf_b), row(cf_g), row(cf_bb), sc_w, row(sc_b), pad_lanes(dt_bias), pad_lanes(a_log),
              row(jnp.repeat(d_skip, HEAD)), row(norm_g)]
    scratch = [
        pltpu.VMEM((CF_HALO + ODD_TILE + 8, WIDTH), F32),
        pltpu.VMEM((8, SSD_XBC), F32),
        pltpu.VMEM((HEADS // 2, SSD_STATE, LANES), F32),
    ]
    return _layer_call(_odd_layer_kernel, "odd_layer", ODD_TILE, x, scale1, shift, gate, norm_pre, w_in,
                       norm_post, w_out, params, scratch)


def kernel(x, c, ada_w, ada_b, norm_pre, norm_post, ev_w_in, ev_w_out, tm_mu, tm_w0, tm_w2, tm_a0, tm_a2,
           tm_k_k, tm_k_a, tm_r_k, tm_lnx_g, tm_lnx_b, sc_conv_w, od_w_in, od_w_out, cf_conv_w, cf_conv_b,
           cf_ln_g, cf_ln_b, ssd_conv_w, ssd_conv_b, ssd_dt_bias, ssd_a_log, ssd_d, ssd_norm_g):
    depth = ada_w.shape[0]
    bsz, t, d = x.shape
    assert d == D_MODEL and t % EVEN_TILE == 0 and t % ODD_TILE == 0
    mod = _ada_modulation(c, ada_w, ada_b)
    od_cols = od_w_in.shape[-1]
    od_w_pad = jnp.zeros(od_w_in.shape[:2] + (ODD_COLS_PAD,), BF16).at[..., :od_cols].set(od_w_in.astype(BF16))
    for i in range(depth):
        shift = mod[i, :, None, 0:d]
        scale1 = 1.0 + mod[i, :, None, d:2 * d]
        gate = mod[i, :, None, 2 * d:3 * d]
        j = i // 2
        if i % 2 == 0:
            x = _even_layer(x, scale1, shift, gate, norm_pre[i], ev_w_in[j].astype(BF16), norm_post[i],
                            ev_w_out[j].astype(BF16), tm_mu[j], tm_w0[j], tm_w2[j], tm_a0[j], tm_a2[j],
                            tm_k_k[j].reshape(-1), tm_k_a[j].reshape(-1), tm_r_k[j].reshape(-1), tm_lnx_g[j],
                            tm_lnx_b[j], sc_conv_w[j])
        else:
            x = _odd_layer(x, scale1, shift, gate, norm_pre[i], od_w_pad[j], norm_post[i],
                           od_w_out[j].astype(BF16), cf_conv_w[j], cf_conv_b[j], cf_ln_g[j], cf_ln_b[j],
                           ssd_conv_w[j], ssd_conv_b[j], ssd_dt_bias[j], ssd_a_log[j], ssd_d[j], ssd_norm_g[j])
    return x
```

```python
import functools

import jax
import jax.numpy as jnp
from jax import lax
from jax.experimental import pallas as pl
from jax.experimental.pallas import tpu as pltpu

F32 = jnp.float32
BF16 = jnp.bfloat16

D_MODEL = 1024
CHUNK = 64
EVEN_TILE = 512
ODD_TILE = 512
HEAD = 64
HEADS = 8
WIDTH = 512
LORA = 64
TM_COLS = 4 * WIDTH + 2 * LORA
EVEN_COLS = TM_COLS + 4 * WIDTH
SSD_STATE = 128
SSD_GROUPS = 2
SSD_HPG = HEADS // SSD_GROUPS
SSD_XBC = WIDTH + 2 * SSD_GROUPS * SSD_STATE
CF_KERNEL = 31
CF_HALO = 32
LANES = 128
ODD_COLS_PAD = 3 * WIDTH + WIDTH + SSD_XBC + LANES
NORM_EPS = 1e-6
TM_LN_EPS = 64e-5
CF_LN_EPS = 1e-5
VMEM_LIMIT = 56 * 1024 * 1024


def _dot(a, b):
    return jnp.dot(a.astype(BF16), b.astype(BF16), preferred_element_type=F32)


def _dot_nt(a, b):
    return lax.dot_general(a.astype(BF16), b.astype(BF16), (((1,), (1,)), ((), ())),
                           preferred_element_type=F32)


def _dot_tn(a, b):
    return lax.dot_general(a.astype(BF16), b.astype(BF16), (((0,), (0,)), ((), ())),
                           preferred_element_type=F32)


def _split_hi_lo(x):
    hi = x.astype(BF16)
    lo = (x - hi.astype(F32)).astype(BF16)
    return hi, lo


def _dot_exact_lhs(m, x):
    hi, lo = _split_hi_lo(x)
    mb = m.astype(BF16)
    return (jnp.dot(mb, hi, preferred_element_type=F32) +
            jnp.dot(mb, lo, preferred_element_type=F32))


def _chunk_cumsum(x):
    ii = lax.broadcasted_iota(jnp.int32, (CHUNK, CHUNK), 0)
    jj = lax.broadcasted_iota(jnp.int32, (CHUNK, CHUNK), 1)
    tri = (ii >= jj).astype(BF16)
    hi, lo = _split_hi_lo(x)
    blocks = [jnp.dot(tri, hi[s:s + CHUNK], preferred_element_type=F32) +
              jnp.dot(tri, lo[s:s + CHUNK], preferred_element_type=F32) for s in range(0, x.shape[0], CHUNK)]
    return jnp.concatenate(blocks, axis=0)


def _sigmoid(x):
    return 0.5 * jnp.tanh(0.5 * x) + 0.5


def _silu(x):
    h = 0.5 * x
    return h + h * jnp.tanh(h)


def _softplus(x):
    return jnp.maximum(x, 0.0) + jnp.log(1.0 + jnp.exp(-jnp.abs(x)))


def _shift_rows(x, carry, d):
    out = pltpu.roll(x, d, 0)
    row = lax.broadcasted_iota(jnp.int32, (8, 1), 0)
    head = jnp.where(row < d, pltpu.roll(carry, d, 0), out[:8])
    return jnp.concatenate([head, out[8:]], axis=0)


def _ada_kernel(c_ref, w_ref, b_ref, o_ref):
    ca = _silu(c_ref[...])
    o_ref[...] = jnp.dot(ca, w_ref[...], preferred_element_type=F32,
                         precision=lax.Precision.HIGHEST) + b_ref[...]


def _ada_modulation(c, ada_w, ada_b):
    depth, d, d3 = ada_w.shape
    bsz = c.shape[0]
    rows = 8
    c_pad = jnp.zeros((rows, d), F32).at[:bsz].set(c)
    tn = 768
    out = pl.pallas_call(
        _ada_kernel,
        grid=(depth, d3 // tn),
        in_specs=[
            pl.BlockSpec((rows, d), lambda i, j: (0, 0)),
            pl.BlockSpec((None, d, tn), lambda i, j: (i, 0, j)),
            pl.BlockSpec((None, 1, tn), lambda i, j: (i, 0, j)),
        ],
        out_specs=pl.BlockSpec((None, rows, tn), lambda i, j: (i, 0, j)),
        out_shape=jax.ShapeDtypeStruct((depth, rows, d3), F32),
        compiler_params=pltpu.CompilerParams(
            dimension_semantics=("parallel", "parallel"), vmem_limit_bytes=VMEM_LIMIT),
        name="ada_modulation",
    )(c_pad, ada_w, ada_b.reshape(depth, 1, d3))
    return out[:, :bsz]


PIPE_LAG = 2


PROJ_BLOCK = 3 * LANES


def _projection_emitter(xn_ref, npre_ref, sc_ref, sh_ref, win_ref, p_ref, block=PROJ_BLOCK):
    x = xn_ref[...]
    ms = jnp.mean(x * x, axis=-1, keepdims=True)
    h = x * lax.rsqrt(ms + NORM_EPS) * npre_ref[...]
    h = (h * sc_ref[...] + sh_ref[...]).astype(BF16)
    cols = win_ref.shape[1]
    starts = list(range(0, cols, block))

    def emit(k=None):
        blk = None
        for _ in range(len(starts) if k is None else min(k, len(starts))):
            lo = starts.pop(0)
            hi = min(lo + block, cols)
            blk = jnp.dot(h, win_ref[:, lo:hi], preferred_element_type=F32)
            p_ref[:, lo:hi] = blk
        return blk

    return emit


def _zero_after(dep, width):
    bits = lax.bitcast_convert_type(dep[0:8, 0:LANES], jnp.int32)
    zero = lax.shift_right_logical(lax.shift_right_logical(bits, 16), 16).astype(F32)
    return jnp.concatenate([zero[0:1, :]] * (width // LANES), axis=1)


def _residual_update(y_ref, x_ref, wout_ref, npost_ref, gate_ref, o_ref):
    yo = jnp.dot(y_ref[...], wout_ref[...], preferred_element_type=F32)
    ms = jnp.mean(yo * yo, axis=-1, keepdims=True)
    o_ref[...] = x_ref[...] + gate_ref[...] * (yo * lax.rsqrt(ms + NORM_EPS) * npost_ref[...])


def _layer_call(body, name, tile, x, scale1, shift, gate, norm_pre, w_in, norm_post, w_out, params, scratch):
    bsz, t, d = x.shape
    n_tiles = t // tile
    last = bsz * n_tiles - 1
    cols = w_in.shape[1]
    row = lambda a: a.reshape(1, -1)
    full = lambda a: pl.BlockSpec(a.shape, lambda s: (0,) * a.ndim)
    tile_in = lambda s: jnp.minimum(s, last)
    tile_out = lambda s: jnp.maximum(s - PIPE_LAG, 0)
    xn_spec = pl.BlockSpec((tile, d), lambda s: (tile_in(s), 0))
    xr_spec = pl.BlockSpec((tile, d), lambda s: (tile_out(s), 0))
    mod_in = pl.BlockSpec((None, 1, d), lambda s: (tile_in(s) // n_tiles, 0, 0))
    mod_out = pl.BlockSpec((None, 1, d), lambda s: (tile_out(s) // n_tiles, 0, 0))
    shared = [row(norm_pre), w_in, row(norm_post), w_out] + list(params)
    x2 = x.reshape(bsz * t, d)
    out = pl.pallas_call(
        functools.partial(body, n_tiles=n_tiles),
        grid=(bsz * n_tiles + PIPE_LAG,),
        in_specs=[xn_spec, xr_spec, mod_in, mod_in, mod_out] + [full(a) for a in shared],
        out_specs=xr_spec,
        out_shape=jax.ShapeDtypeStruct((bsz * t, d), F32),
        scratch_shapes=[
            pltpu.VMEM((tile, cols), F32),
            pltpu.VMEM((tile, 2 * WIDTH), BF16),
        ] + list(scratch),
        compiler_params=pltpu.CompilerParams(
            dimension_semantics=("arbitrary",), vmem_limit_bytes=VMEM_LIMIT),
        name=name,
    )(x2, x2, scale1, shift, gate, *shared)
    return out.reshape(bsz, t, d)


def _bd(x):
    lane = lax.broadcasted_iota(jnp.int32, x.shape, 1)
    return jnp.concatenate([jnp.where(lane < HEAD, x, 0.0), jnp.where(lane >= HEAD, x, 0.0)],
                           axis=0).astype(BF16)


def _bd2(z):
    return jnp.concatenate([_bd(z[:, :LANES]), _bd(z[:, LANES:])], axis=1)


def _tn_pair(x, y):
    full = _dot_tn(x, y)
    lane = lax.broadcasted_iota(jnp.int32, (HEAD, LANES), 1)
    return jnp.where(lane < HEAD, full[:HEAD], full[HEAD:])


def _seg_sum(x, ones_bd):
    rows = x.shape[0]
    n = x.shape[1] // LANES
    xs = jnp.concatenate([x[:, j * LANES:(j + 1) * LANES] for j in range(n)], axis=0)
    s = jnp.dot(xs.astype(BF16), ones_bd, preferred_element_type=F32)
    return jnp.concatenate([s[j * rows:(j + 1) * rows] for j in range(n)], axis=1)


def _even_layer_kernel(xn_ref, xr_ref, sc_ref, sh_ref, gate_ref, npre_ref, win_ref, npost_ref, wout_ref,
                       mu_ref, w0_ref, w2_ref, a0_ref, a2_ref, kk_ref, ka_ref, rk_ref, lng_ref, lnb_ref, cw_ref,
                       o_ref, pbuf_ref, y_ref, prow_ref, ht_ref, uc_ref, *, n_tiles):
    L = CHUNK
    T = xn_ref.shape[0]
    nsub = T // L
    W = WIDTH
    cat = jnp.concatenate
    step = pl.program_id(0)

    @pl.when(step == 0)
    def _():
        pbuf_ref[...] = jnp.zeros_like(pbuf_ref)
        y_ref[...] = jnp.zeros_like(y_ref)

    @pl.when((step == 0) | ((step + n_tiles - 1) % n_tiles == 0))
    def _():
        prow_ref[...] = jnp.zeros_like(prow_ref)
        ht_ref[...] = jnp.zeros_like(ht_ref)
        uc_ref[...] = jnp.zeros_like(uc_ref)

    _residual_update(y_ref, xr_ref, wout_ref, npost_ref, gate_ref, o_ref)

    p_ref = pbuf_ref
    p_tm = p_ref[:, :TM_COLS]
    o = TM_COLS
    b_gate = p_ref[:, o:o + W]
    c_gate = p_ref[:, o + W:o + 2 * W]
    hh = p_ref[:, o + 2 * W:o + 3 * W]
    g2 = p_ref[:, o + 3 * W:o + 4 * W]
    emit_proj = _projection_emitter(xn_ref, npre_ref, sc_ref, sh_ref, win_ref, pbuf_ref, block=4 * LANES)

    blk = emit_proj(1)

    u = c_gate * hh
    carry = uc_ref[...]
    conv = (cw_ref[0:1, :] * _shift_rows(u, carry, 2) + cw_ref[1:2, :] * _shift_rows(u, carry, 1) +
            cw_ref[2:3, :] * u)
    uc_ref[...] = u[T - 8:T, :]
    y_ref[:, W:] = (b_gate * conv * _silu(g2)).astype(BF16)

    prev = _shift_rows(p_tm, prow_ref[...], 1)
    prow_ref[...] = p_tm[T - 8:T, :]
    ps = p_tm + (prev - p_tm) * (mu_ref[...] + _zero_after(blk, TM_COLS))
    blk = emit_proj(1)
    r = ps[:, 0:W]
    k = ps[:, W:2 * W]
    v = ps[:, 2 * W:3 * W]
    g = ps[:, 3 * W:4 * W]
    wd = ps[:, 4 * W:4 * W + LORA]
    ad = ps[:, 4 * W + LORA:4 * W + 2 * LORA]
    w_log = -_softplus(-(w0_ref[...] + _zero_after(blk, W) + _dot(jnp.tanh(wd), w2_ref[...]))) - 0.5
    lw = -jnp.exp(w_log)
    blk = emit_proj(1)
    a_icl = _sigmoid(a0_ref[...] + _zero_after(blk, W) + _dot(ad, a2_ref[...]))
    k2 = k * (1.0 + (a_icl - 1.0) * ka_ref[...])
    ones_bd = ((lax.broadcasted_iota(jnp.int32, (LANES, LANES), 0) >> 6) ==
               (lax.broadcasted_iota(jnp.int32, (LANES, LANES), 1) >> 6)).astype(BF16)
    kk = k * kk_ref[...]
    kk = kk * lax.rsqrt(jnp.maximum(_seg_sum(kk * kk, ones_bd), 1e-24))
    kb = kk * a_icl
    blk = emit_proj(2)

    cum_incl = _chunk_cumsum(lw)
    cum_excl = cum_incl - lw
    rows_of = lambda x, i: x[i:i + 1, :]
    mid_rows = [rows_of(cum_incl, s * L + L // 2 - 1) for s in range(nsub)]
    end_rows = [rows_of(cum_incl, s * L + L - 1) for s in range(nsub)]
    mid = cat([jnp.broadcast_to(m, (L, W)) for m in mid_rows], axis=0)
    end = cat([jnp.broadcast_to(m, (L, W)) for m in end_rows], axis=0)
    g_mid = [jnp.exp(m) for m in mid_rows]
    g_end = [jnp.exp(m) for m in end_rows]
    e_k = jnp.exp(mid - cum_incl)
    e_end = jnp.exp(end - cum_incl)
    blk = emit_proj(2)
    at_all = -kk * jnp.exp(cum_excl - mid)
    rt_all = r * jnp.exp(cum_incl - mid)
    bt_all = kb * e_k
    kt_all = k2 * e_k
    bh_all = kb * e_end
    kh_all = k2 * e_end
    emit_proj()

    ri = lax.broadcasted_iota(jnp.int32, (L, LANES), 0)
    cj = lax.broadcasted_iota(jnp.int32, (L, LANES), 1) & (HEAD - 1)
    eye = (ri == cj).astype(F32)
    strict = ri > cj
    same4 = (ri >> 2) == (cj >> 2)
    same16 = (ri >> 4) == (cj >> 4)
    m0 = strict & same4
    m1 = strict & same16 & jnp.logical_not(same4)
    m2 = strict & jnp.logical_not(same16)
    r2 = lax.broadcasted_iota(jnp.int32, (2 * L, 2 * LANES), 0)
    c2 = lax.broadcasted_iota(jnp.int32, (2 * L, 2 * LANES), 1) & (HEAD - 1)
    mask_aa = (r2 & (L - 1)) >= jnp.where(r2 < L, c2 + 1, c2)
    zero_sq = jnp.zeros((LANES, LANES), BF16)

    cs = [(s, pr) for s in range(nsub) for pr in range(HEADS // 2)]
    tile = lambda x, c: x[c[0] * L:(c[0] + 1) * L, c[1] * LANES:(c[1] + 1) * LANES]
    at = [tile(at_all, c) for c in cs]
    rt = [tile(rt_all, c) for c in cs]
    vv = [tile(v, c) for c in cs]
    bh = [tile(bh_all, c) for c in cs]
    kh = [tile(kh_all, c) for c in cs]
    n = range(len(cs))

    aa = [_dot_nt(cat([at[i], rt[i]], axis=0), cat([_bd(tile(bt_all, c)), _bd(tile(kt_all, c))], axis=0))
          for i, c in enumerate(cs)]
    aa = [jnp.where(mask_aa, x, 0.0) for x in aa]
    n_ab = [x[:L, :LANES] for x in aa]
    akv = [_dot(aa[i][:L, LANES:], _bd(vv[i])) for i in n]

    n0 = [jnp.where(m0, x, 0.0) for x in n_ab]
    n0sq = [_dot(x, _bd(x)) for x in n0]
    d0 = [eye + n0[i] + n0sq[i] + _dot(n0[i], _bd(n0sq[i])) for i in n]
    p1 = [_dot(d0[i], _bd(jnp.where(m1, n_ab[i], 0.0))) for i in n]
    pm = [_dot(p1[i], cat([_bd(p1[i]), _bd(d0[i])], axis=1)) for i in n]
    e1 = [d0[i] + pm[i][:, LANES:] for i in n]
    d1 = [e1[i] + _dot(pm[i][:, :LANES], _bd(e1[i])) for i in n]
    dm = [_dot(d1[i], cat([_bd(jnp.where(m2, n_ab[i], 0.0)), _bd(at[i]), _bd(akv[i])], axis=1))
          for i in n]
    p2 = [x[:, :LANES] for x in dm]
    z = [x[:, LANES:] for x in dm]
    qm = [_dot(p2[i], cat([_bd(p2[i]), _bd2(z[i])], axis=1)) for i in n]
    z = [z[i] + qm[i][:, LANES:] for i in n]
    z = [z[i] + _dot(qm[i][:, :LANES], _bd2(z[i])) for i in n]

    top = [_dot(aa[i][L:, :], cat([_bd2(z[i]), cat([zero_sq, _bd(vv[i])], axis=1)], axis=0))
           for i in n]
    gm = [g_mid[c[0]][:, c[1] * LANES:(c[1] + 1) * LANES] for c in cs]
    ge = [g_end[c[0]][:, c[1] * LANES:(c[1] + 1) * LANES] for c in cs]
    q_eff = [(rt[i] + top[i][:, :LANES]) * gm[i] for i in n]
    m_low = [_tn_pair(bh[i], z[i][:, :LANES]) * gm[i] for i in n]
    g_t = [_tn_pair(cat([z[i][:, LANES:], vv[i]], axis=0), cat([bh[i], kh[i]], axis=0)) for i in n]

    state = [ht_ref[pr] for pr in range(HEADS // 2)]
    y_tiles = []
    for i, (s, pr) in enumerate(cs):
        y_tiles.append(_dot_nt(q_eff[i], _bd(state[pr])) + top[i][:, LANES:])
        state[pr] = state[pr] * ge[i] + _dot_nt(state[pr], _bd(m_low[i])) + g_t[i]
    for pr in range(HEADS // 2):
        ht_ref[pr] = state[pr]

    npr = HEADS // 2
    y = cat([cat(y_tiles[s * npr:(s + 1) * npr], axis=1) for s in range(nsub)], axis=0)
    inv_n = 1.0 / HEAD
    yc = y - _seg_sum(y, ones_bd) * inv_n
    emit_proj(1)
    var = _seg_sum(yc * yc, ones_bd) * inv_n
    y_n = yc * lax.rsqrt(var + TM_LN_EPS) * lng_ref[...] + lnb_ref[...]
    emit_proj(1)
    bonus = _seg_sum(r * k2 * rk_ref[...], ones_bd) * v
    y_ref[:, :W] = ((y_n + bonus) * _silu(g)).astype(BF16)
    emit_proj()


def _even_layer(x, scale1, shift, gate, norm_pre, w_in, norm_post, w_out, mu, w0, w2, a0, a2, k_k, k_a, r_k,
                lnx_g, lnx_b, conv_w):
    row = lambda a: a.reshape(1, -1)
    params = [row(mu), row(w0), w2, row(a0), a2, row(k_k), row(k_a), row(r_k), row(lnx_g), row(lnx_b), conv_w]
    scratch = [
        pltpu.VMEM((8, TM_COLS), F32),
        pltpu.VMEM((HEADS // 2, HEAD, LANES), F32),
        pltpu.VMEM((8, WIDTH), F32),
    ]
    return _layer_call(_even_layer_kernel, "even_layer", EVEN_TILE, x, scale1, shift, gate, norm_pre, w_in,
                       norm_post, w_out, params, scratch)


def _odd_layer_kernel(xn_ref, xr_ref, sc_ref, sh_ref, gate_ref, npre_ref, win_ref, npost_ref, wout_ref,
                      cfw_ref, cfb_ref, cfg_ref, cfbb_ref, scw_ref, scb_ref, dtb_ref, alog_ref, dsk_ref, ng_ref,
                      o_ref, pbuf_ref, y_ref, ubuf_ref, xc_ref, st_ref, *, n_tiles):
    L = CHUNK
    T = xn_ref.shape[0]
    nsub = T // L
    W = WIDTH
    cat = jnp.concatenate
    step = pl.program_id(0)

    @pl.when(step == 0)
    def _():
        pbuf_ref[...] = jnp.zeros_like(pbuf_ref)
        y_ref[...] = jnp.zeros_like(y_ref)

    @pl.when((step == 0) | ((step + n_tiles - 1) % n_tiles == 0))
    def _():
        ubuf_ref[...] = jnp.zeros_like(ubuf_ref)
        xc_ref[...] = jnp.zeros_like(xc_ref)
        st_ref[...] = jnp.zeros_like(st_ref)

    _residual_update(y_ref, xr_ref, wout_ref, npost_ref, gate_ref, o_ref)

    p_ref = pbuf_ref
    val = p_ref[:, 0:W]
    glu = p_ref[:, W:2 * W]
    g = p_ref[:, 2 * W:3 * W]
    o = 3 * W
    z = p_ref[:, o:o + W]
    xbc_in = p_ref[:, o + W:o + W + SSD_XBC]
    dt_in = p_ref[:, o + W + SSD_XBC:o + W + SSD_XBC + LANES]
    emit_proj = _projection_emitter(xn_ref, npre_ref, sc_ref, sh_ref, win_ref, pbuf_ref, block=4 * LANES)

    ubuf_ref[CF_HALO:CF_HALO + T, :] = val * _sigmoid(glu)
    first = CF_HALO - (CF_KERNEL - 1)
    acc = jnp.zeros((T, W), F32) + cfb_ref[...]
    proj_blk = None
    for rr in range(8):
        part = None
        for i in range(CF_KERNEL):
            off = first + i
            if off % 8 != rr:
                continue
            w_row = cfw_ref[i:i + 1, :]
            if part is None and proj_blk is not None:
                w_row = w_row + _zero_after(proj_blk, W)
            term = w_row * ubuf_ref[off - rr:off - rr + T + 8, :]
            part = term if part is None else part + term
        if rr == 0:
            acc = acc + part[:T]
        else:
            acc = acc + pltpu.roll(part, T + 8 - rr, 0)[:T]
        proj_blk = emit_proj(1)
    ubuf_ref[:CF_HALO, :] = ubuf_ref[T:T + CF_HALO, :]
    mean = jnp.mean(acc, axis=-1, keepdims=True)
    ac = acc - mean
    var = jnp.mean(ac * ac, axis=-1, keepdims=True)
    ln = ac * lax.rsqrt(var + CF_LN_EPS) * cfg_ref[...] + cfbb_ref[...]
    y_ref[:, :W] = (_silu(ln) * _silu(g)).astype(BF16)
    emit_proj(1)

    carry = xc_ref[...]
    conv = (scw_ref[0:1, :] * _shift_rows(xbc_in, carry, 3) + scw_ref[1:2, :] * _shift_rows(xbc_in, carry, 2) +
            scw_ref[2:3, :] * _shift_rows(xbc_in, carry, 1) + scw_ref[3:4, :] * xbc_in + scb_ref[...])
    xc_ref[...] = xbc_in[T - 8:T, :]
    emit_proj(1)
    xbc = _silu(conv)
    emit_proj(1)
    xs = xbc[:, :W]
    dt = _softplus(dt_in + dtb_ref[...])
    da = dt * (-jnp.exp(alog_ref[...]))
    cs = _chunk_cumsum(da)
    expand = ((lax.broadcasted_iota(jnp.int32, (LANES, W), 1) >> 6) ==
              lax.broadcasted_iota(jnp.int32, (LANES, W), 0)).astype(BF16)

    def widen(x):
        hi, lo = _split_hi_lo(x)
        return (jnp.dot(hi, expand, preferred_element_type=F32) + jnp.dot(lo, expand, preferred_element_type=F32))

    cs_w = widen(cs)
    x_dt = xs * widen(dt)
    ri = lax.broadcasted_iota(jnp.int32, (L, W), 0)
    cj = lax.broadcasted_iota(jnp.int32, (L, W), 1) & (HEAD - 1)
    diag = (ri == cj).astype(F32)
    causal = ri >= cj
    ones_ll = jnp.ones((L, L), F32)

    y_chunks = []
    state = [st_ref[pr] for pr in range(HEADS // 2)]
    for s in range(nsub):
        rows = slice(s * L, (s + 1) * L)
        csw = cs_w[rows]
        last = csw[L - 1:L, :]
        row_b = _dot_exact_lhs(ones_ll, csw * diag)
        lmat = jnp.exp(jnp.where(causal, csw - row_b, -1e30))
        xd = x_dt[rows]
        xd_dec = xd * jnp.exp(last - csw)
        e_cs = jnp.exp(csw)
        e_last = jnp.exp(last)
        tiles = []
        for grp in range(SSD_GROUPS):
            bm = xbc[rows, W + grp * SSD_STATE:W + (grp + 1) * SSD_STATE]
            cm = xbc[rows, W + (SSD_GROUPS + grp) * SSD_STATE:W + (SSD_GROUPS + grp + 1) * SSD_STATE]
            cb2 = _dot_nt(cm, cat([bm, bm], axis=0))
            for e in range(SSD_HPG // 2):
                pr = grp * (SSD_HPG // 2) + e
                lanes = slice(pr * LANES, (pr + 1) * LANES)
                y_diag = _dot(cb2 * lmat[:, lanes], _bd(xd[:, lanes]))
                y_off = _dot(cm, state[pr]) * e_cs[:, lanes]
                state[pr] = state[pr] * e_last[:, lanes] + _dot_tn(bm, xd_dec[:, lanes])
                tiles.append(y_diag + y_off)
        y_chunks.append(cat(tiles, axis=1))
    for pr in range(HEADS // 2):
        st_ref[pr] = state[pr]
    y = cat(y_chunks, axis=0) + xs * dsk_ref[...]
    y = y * _silu(z)
    ms = jnp.mean(y * y, axis=-1, keepdims=True)
    y_ref[:, W:] = (y * lax.rsqrt(ms + NORM_EPS) * ng_ref[...]).astype(BF16)
    emit_proj()


def _odd_layer(x, scale1, shift, gate, norm_pre, w_in, norm_post, w_out, cf_w, cf_b, cf_g, cf_bb, sc_w, sc_b,
               dt_bias, a_log, d_skip, norm_g):
    row = lambda a: a.reshape(1, -1)
    pad_lanes = lambda a: jnp.zeros((1, LANES), F32).at[0, :a.shape[0]].set(a)
    params = [cf_w, row(cf_b), row(cf_g), row(cf_bb), sc_w, row(sc_b), pad_lanes(dt_bias), pad_lanes(a_log),
              row(jnp.repeat(d_skip, HEAD)), row(norm_g)]
    scratch = [
        pltpu.VMEM((CF_HALO + ODD_TILE + 8, WIDTH), F32),
        pltpu.VMEM((8, SSD_XBC), F32),
        pltpu.VMEM((HEADS // 2, SSD_STATE, LANES), F32),
    ]
    return _layer_call(_odd_layer_kernel, "odd_layer", ODD_TILE, x, scale1, shift, gate, norm_pre, w_in,
                       norm_post, w_out, params, scratch)


def kernel(x, c, ada_w, ada_b, norm_pre, norm_post, ev_w_in, ev_w_out, tm_mu, tm_w0, tm_w2, tm_a0, tm_a2,
           tm_k_k, tm_k_a, tm_r_k, tm_lnx_g, tm_lnx_b, sc_conv_w, od_w_in, od_w_out, cf_conv_w, cf_conv_b,
           cf_ln_g, cf_ln_b, ssd_conv_w, ssd_conv_b, ssd_dt_bias, ssd_a_log, ssd_d, ssd_norm_g):
    depth = ada_w.shape[0]
    bsz, t, d = x.shape
    assert d == D_MODEL and t % EVEN_TILE == 0 and t % ODD_TILE == 0
    mod = _ada_modulation(c, ada_w, ada_b)
    od_cols = od_w_in.shape[-1]
    od_w_pad = jnp.zeros(od_w_in.shape[:2] + (ODD_COLS_PAD,), BF16).at[..., :od_cols].set(od_w_in.astype(BF16))
    for i in range(depth):
        shift = mod[i, :, None, 0:d]
        scale1 = 1.0 + mod[i, :, None, d:2 * d]
        gate = mod[i, :, None, 2 * d:3 * d]
        j = i // 2
        if i % 2 == 0:
            x = _even_layer(x, scale1, shift, gate, norm_pre[i], ev_w_in[j].astype(BF16), norm_post[i],
                            ev_w_out[j].astype(BF16), tm_mu[j], tm_w0[j], tm_w2[j], tm_a0[j], tm_a2[j],
                            tm_k_k[j].reshape(-1), tm_k_a[j].reshape(-1), tm_r_k[j].reshape(-1), tm_lnx_g[j],
                            tm_lnx_b[j], sc_conv_w[j])
        else:
            x = _odd_layer(x, scale1, shift, gate, norm_pre[i], od_w_pad[j], norm_post[i],
                           od_w_out[j].astype(BF16), cf_conv_w[j], cf_conv_b[j], cf_ln_g[j], cf_ln_b[j],
                           ssd_conv_w[j], ssd_conv_b[j], ssd_dt_bias[j], ssd_a_log[j], ssd_d[j], ssd_norm_g[j])
    return x
```

```python
import functools

import jax
import jax.numpy as jnp
from jax import lax
from jax.experimental import pallas as pl
from jax.experimental.pallas import tpu as pltpu

F32 = jnp.float32
BF16 = jnp.bfloat16

D_MODEL = 1024
CHUNK = 64
EVEN_TILE = 512
ODD_TILE = 512
HEAD = 64
HEADS = 8
WIDTH = 512
LORA = 64
TM_COLS = 4 * WIDTH + 2 * LORA
EVEN_COLS = TM_COLS + 4 * WIDTH
SSD_STATE = 128
SSD_GROUPS = 2
SSD_HPG = HEADS // SSD_GROUPS
SSD_XBC = WIDTH + 2 * SSD_GROUPS * SSD_STATE
CF_KERNEL = 31
CF_HALO = 32
LANES = 128
ODD_COLS_PAD = 3 * WIDTH + WIDTH + SSD_XBC + LANES
NORM_EPS = 1e-6
TM_LN_EPS = 64e-5
CF_LN_EPS = 1e-5
VMEM_LIMIT = 56 * 1024 * 1024


def _dot(a, b):
    return jnp.dot(a.astype(BF16), b.astype(BF16), preferred_element_type=F32)


def _dot_nt(a, b):
    return lax.dot_general(a.astype(BF16), b.astype(BF16), (((1,), (1,)), ((), ())),
                           preferred_element_type=F32)


def _dot_tn(a, b):
    return lax.dot_general(a.astype(BF16), b.astype(BF16), (((0,), (0,)), ((), ())),
                           preferred_element_type=F32)


def _split_hi_lo(x):
    hi = x.astype(BF16)
    lo = (x - hi.astype(F32)).astype(BF16)
    return hi, lo


def _dot_exact_lhs(m, x):
    hi, lo = _split_hi_lo(x)
    mb = m.astype(BF16)
    return (jnp.dot(mb, hi, preferred_element_type=F32) +
            jnp.dot(mb, lo, preferred_element_type=F32))


def _chunk_cumsum(x):
    ii = lax.broadcasted_iota(jnp.int32, (CHUNK, CHUNK), 0)
    jj = lax.broadcasted_iota(jnp.int32, (CHUNK, CHUNK), 1)
    tri = (ii >= jj).astype(BF16)
    hi, lo = _split_hi_lo(x)
    blocks = [jnp.dot(tri, hi[s:s + CHUNK], preferred_element_type=F32) +
              jnp.dot(tri, lo[s:s + CHUNK], preferred_element_type=F32) for s in range(0, x.shape[0], CHUNK)]
    return jnp.concatenate(blocks, axis=0)


def _sigmoid(x):
    return 0.5 * jnp.tanh(0.5 * x) + 0.5


def _silu(x):
    h = 0.5 * x
    return h + h * jnp.tanh(h)


def _softplus(x):
    return jnp.maximum(x, 0.0) + jnp.log(1.0 + jnp.exp(-jnp.abs(x)))


def _shift_rows(x, carry, d):
    out = pltpu.roll(x, d, 0)
    row = lax.broadcasted_iota(jnp.int32, (8, 1), 0)
    head = jnp.where(row < d, pltpu.roll(carry, d, 0), out[:8])
    return jnp.concatenate([head, out[8:]], axis=0)


def _ada_kernel(c_ref, w_ref, b_ref, o_ref):
    hi, lo = _split_hi_lo(_silu(c_ref[...]))
    wb = w_ref[...].astype(BF16)
    o_ref[...] = (jnp.dot(hi, wb, preferred_element_type=F32) + jnp.dot(lo, wb, preferred_element_type=F32) +
                  b_ref[...])


def _ada_modulation(c, ada_w, ada_b):
    depth, d, d3 = ada_w.shape
    bsz = c.shape[0]
    rows = 16
    c_pad = jnp.zeros((rows, d), F32).at[:bsz].set(c)
    tn = 1536
    out = pl.pallas_call(
        _ada_kernel,
        grid=(depth, d3 // tn),
        in_specs=[
            pl.BlockSpec((rows, d), lambda i, j: (0, 0)),
            pl.BlockSpec((None, d, tn), lambda i, j: (i, 0, j)),
            pl.BlockSpec((None, 1, tn), lambda i, j: (i, 0, j)),
        ],
        out_specs=pl.BlockSpec((None, rows, tn), lambda i, j: (i, 0, j)),
        out_shape=jax.ShapeDtypeStruct((depth, rows, d3), F32),
        compiler_params=pltpu.CompilerParams(
            dimension_semantics=("parallel", "parallel"), vmem_limit_bytes=VMEM_LIMIT),
        name="ada_modulation",
    )(c_pad, ada_w, ada_b.reshape(depth, 1, d3))
    return out[:, :bsz]


PIPE_LAG = 2


PROJ_BLOCK = 3 * LANES


def _projection_emitter(xn_ref, npre_ref, sc_ref, sh_ref, win_ref, p_ref, block=PROJ_BLOCK):
    x = xn_ref[...]
    ms = jnp.mean(x * x, axis=-1, keepdims=True)
    h = x * lax.rsqrt(ms + NORM_EPS) * npre_ref[...]
    h = (h * sc_ref[...] + sh_ref[...]).astype(BF16)
    cols = win_ref.shape[1]
    starts = list(range(0, cols, block))

    def emit(k=None):
        blk = None
        for _ in range(len(starts) if k is None else min(k, len(starts))):
            lo = starts.pop(0)
            hi = min(lo + block, cols)
            blk = jnp.dot(h, win_ref[:, lo:hi], preferred_element_type=F32)
            p_ref[:, lo:hi] = blk
        return blk

    return emit


def _zero_after(dep, width):
    bits = lax.bitcast_convert_type(dep[0:8, 0:LANES], jnp.int32)
    zero = lax.shift_right_logical(lax.shift_right_logical(bits, 16), 16).astype(F32)
    return jnp.concatenate([zero[0:1, :]] * (width // LANES), axis=1)


def _residual_update(y_ref, x_ref, wout_ref, npost_ref, gate_ref, o_ref):
    yo = jnp.dot(y_ref[...], wout_ref[...], preferred_element_type=F32)
    ms = jnp.mean(yo * yo, axis=-1, keepdims=True)
    o_ref[...] = x_ref[...] + gate_ref[...] * (yo * lax.rsqrt(ms + NORM_EPS) * npost_ref[...])


def _layer_call(body, name, tile, x, scale1, shift, gate, norm_pre, w_in, norm_post, w_out, params, scratch):
    bsz, t, d = x.shape
    n_tiles = t // tile
    last = bsz * n_tiles - 1
    cols = w_in.shape[1]
    row = lambda a: a.reshape(1, -1)
    full = lambda a: pl.BlockSpec(a.shape, lambda s: (0,) * a.ndim)
    tile_in = lambda s: jnp.minimum(s, last)
    tile_out = lambda s: jnp.maximum(s - PIPE_LAG, 0)
    xn_spec = pl.BlockSpec((tile, d), lambda s: (tile_in(s), 0))
    xr_spec = pl.BlockSpec((tile, d), lambda s: (tile_out(s), 0))
    mod_in = pl.BlockSpec((None, 1, d), lambda s: (tile_in(s) // n_tiles, 0, 0))
    mod_out = pl.BlockSpec((None, 1, d), lambda s: (tile_out(s) // n_tiles, 0, 0))
    shared = [row(norm_pre), w_in, row(norm_post), w_out] + list(params)
    x2 = x.reshape(bsz * t, d)
    out = pl.pallas_call(
        functools.partial(body, n_tiles=n_tiles),
        grid=(bsz * n_tiles + PIPE_LAG,),
        in_specs=[xn_spec, xr_spec, mod_in, mod_in, mod_out] + [full(a) for a in shared],
        out_specs=xr_spec,
        out_shape=jax.ShapeDtypeStruct((bsz * t, d), F32),
        scratch_shapes=[
            pltpu.VMEM((tile, cols), F32),
            pltpu.VMEM((tile, 2 * WIDTH), BF16),
        ] + list(scratch),
        compiler_params=pltpu.CompilerParams(
            dimension_semantics=("arbitrary",), vmem_limit_bytes=VMEM_LIMIT),
        name=name,
    )(x2, x2, scale1, shift, gate, *shared)
    return out.reshape(bsz, t, d)


def _bd(x):
    lane = lax.broadcasted_iota(jnp.int32, x.shape, 1)
    return jnp.concatenate([jnp.where(lane < HEAD, x, 0.0), jnp.where(lane >= HEAD, x, 0.0)],
                           axis=0).astype(BF16)


def _bd2(z):
    return jnp.concatenate([_bd(z[:, :LANES]), _bd(z[:, LANES:])], axis=1)


def _tn_pair(x, y):
    full = _dot_tn(x, y)
    lane = lax.broadcasted_iota(jnp.int32, (HEAD, LANES), 1)
    return jnp.where(lane < HEAD, full[:HEAD], full[HEAD:])


def _seg_sum(x, ones_bd):
    rows = x.shape[0]
    n = x.shape[1] // LANES
    xs = jnp.concatenate([x[:, j * LANES:(j + 1) * LANES] for j in range(n)], axis=0)
    s = jnp.dot(xs.astype(BF16), ones_bd, preferred_element_type=F32)
    return jnp.concatenate([s[j * rows:(j + 1) * rows] for j in range(n)], axis=1)


def _even_layer_kernel(xn_ref, xr_ref, sc_ref, sh_ref, gate_ref, npre_ref, win_ref, npost_ref, wout_ref,
                       mu_ref, w0_ref, w2_ref, a0_ref, a2_ref, kk_ref, ka_ref, rk_ref, lng_ref, lnb_ref, cw_ref,
                       o_ref, pbuf_ref, y_ref, prow_ref, ht_ref, uc_ref, *, n_tiles):
    L = CHUNK
    T = xn_ref.shape[0]
    nsub = T // L
    W = WIDTH
    cat = jnp.concatenate
    step = pl.program_id(0)

    @pl.when(step == 0)
    def _():
        pbuf_ref[...] = jnp.zeros_like(pbuf_ref)
        y_ref[...] = jnp.zeros_like(y_ref)

    @pl.when((step == 0) | ((step + n_tiles - 1) % n_tiles == 0))
    def _():
        prow_ref[...] = jnp.zeros_like(prow_ref)
        ht_ref[...] = jnp.zeros_like(ht_ref)
        uc_ref[...] = jnp.zeros_like(uc_ref)

    _residual_update(y_ref, xr_ref, wout_ref, npost_ref, gate_ref, o_ref)

    p_ref = pbuf_ref
    p_tm = p_ref[:, :TM_COLS]
    o = TM_COLS
    b_gate = p_ref[:, o:o + W]
    c_gate = p_ref[:, o + W:o + 2 * W]
    hh = p_ref[:, o + 2 * W:o + 3 * W]
    g2 = p_ref[:, o + 3 * W:o + 4 * W]
    emit_proj = _projection_emitter(xn_ref, npre_ref, sc_ref, sh_ref, win_ref, pbuf_ref, block=4 * LANES)

    blk = emit_proj(1)

    u = c_gate * hh
    carry = uc_ref[...]
    conv = (cw_ref[0:1, :] * _shift_rows(u, carry, 2) + cw_ref[1:2, :] * _shift_rows(u, carry, 1) +
            cw_ref[2:3, :] * u)
    uc_ref[...] = u[T - 8:T, :]
    y_ref[:, W:] = (b_gate * conv * _silu(g2)).astype(BF16)

    prev = _shift_rows(p_tm, prow_ref[...], 1)
    prow_ref[...] = p_tm[T - 8:T, :]
    ps = p_tm + (prev - p_tm) * (mu_ref[...] + _zero_after(blk, TM_COLS))
    blk = emit_proj(1)
    r = ps[:, 0:W]
    k = ps[:, W:2 * W]
    v = ps[:, 2 * W:3 * W]
    g = ps[:, 3 * W:4 * W]
    wd = ps[:, 4 * W:4 * W + LORA]
    ad = ps[:, 4 * W + LORA:4 * W + 2 * LORA]
    w_log = -_softplus(-(w0_ref[...] + _zero_after(blk, W) + _dot(jnp.tanh(wd), w2_ref[...]))) - 0.5
    lw = -jnp.exp(w_log)
    blk = emit_proj(1)
    a_icl = _sigmoid(a0_ref[...] + _zero_after(blk, W) + _dot(ad, a2_ref[...]))
    k2 = k * (1.0 + (a_icl - 1.0) * ka_ref[...])
    ones_bd = ((lax.broadcasted_iota(jnp.int32, (LANES, LANES), 0) >> 6) ==
               (lax.broadcasted_iota(jnp.int32, (LANES, LANES), 1) >> 6)).astype(BF16)
    kk = k * kk_ref[...]
    kk = kk * lax.rsqrt(jnp.maximum(_seg_sum(kk * kk, ones_bd), 1e-24))
    kb = kk * a_icl
    blk = emit_proj(2)

    cum_incl = _chunk_cumsum(lw)
    cum_excl = cum_incl - lw
    rows_of = lambda x, i: x[i:i + 1, :]
    mid_rows = [rows_of(cum_incl, s * L + L // 2 - 1) for s in range(nsub)]
    end_rows = [rows_of(cum_incl, s * L + L - 1) for s in range(nsub)]
    mid = cat([jnp.broadcast_to(m, (L, W)) for m in mid_rows], axis=0)
    end = cat([jnp.broadcast_to(m, (L, W)) for m in end_rows], axis=0)
    g_mid = [jnp.exp(m) for m in mid_rows]
    g_end = [jnp.exp(m) for m in end_rows]
    e_k = jnp.exp(mid - cum_incl)
    e_end = jnp.exp(end - cum_incl)
    blk = emit_proj(2)
    at_all = -kk * jnp.exp(cum_excl - mid)
    rt_all = r * jnp.exp(cum_incl - mid)
    bt_all = kb * e_k
    kt_all = k2 * e_k
    bh_all = kb * e_end
    kh_all = k2 * e_end
    emit_proj()

    ri = lax.broadcasted_iota(jnp.int32, (L, LANES), 0)
    cj = lax.broadcasted_iota(jnp.int32, (L, LANES), 1) & (HEAD - 1)
    eye = (ri == cj).astype(F32)
    strict = ri > cj
    same4 = (ri >> 2) == (cj >> 2)
    same16 = (ri >> 4) == (cj >> 4)
    m0 = strict & same4
    m1 = strict & same16 & jnp.logical_not(same4)
    m2 = strict & jnp.logical_not(same16)
    r2 = lax.broadcasted_iota(jnp.int32, (2 * L, 2 * LANES), 0)
    c2 = lax.broadcasted_iota(jnp.int32, (2 * L, 2 * LANES), 1) & (HEAD - 1)
    mask_aa = (r2 & (L - 1)) >= jnp.where(r2 < L, c2 + 1, c2)
    zero_sq = jnp.zeros((LANES, LANES), BF16)

    cs = [(s, pr) for s in range(nsub) for pr in range(HEADS // 2)]
    tile = lambda x, c: x[c[0] * L:(c[0] + 1) * L, c[1] * LANES:(c[1] + 1) * LANES]
    at = [tile(at_all, c) for c in cs]
    rt = [tile(rt_all, c) for c in cs]
    vv = [tile(v, c) for c in cs]
    bh = [tile(bh_all, c) for c in cs]
    kh = [tile(kh_all, c) for c in cs]
    n = range(len(cs))

    aa = [_dot_nt(cat([at[i], rt[i]], axis=0), cat([_bd(tile(bt_all, c)), _bd(tile(kt_all, c))], axis=0))
          for i, c in enumerate(cs)]
    aa = [jnp.where(mask_aa, x, 0.0) for x in aa]
    n_ab = [x[:L, :LANES] for x in aa]
    akv = [_dot(aa[i][:L, LANES:], _bd(vv[i])) for i in n]

    n0 = [jnp.where(m0, x, 0.0) for x in n_ab]
    n0sq = [_dot(x, _bd(x)) for x in n0]
    d0 = [eye + n0[i] + n0sq[i] + _dot(n0[i], _bd(n0sq[i])) for i in n]
    p1 = [_dot(d0[i], _bd(jnp.where(m1, n_ab[i], 0.0))) for i in n]
    pm = [_dot(p1[i], cat([_bd(p1[i]), _bd(d0[i])], axis=1)) for i in n]
    e1 = [d0[i] + pm[i][:, LANES:] for i in n]
    d1 = [e1[i] + _dot(pm[i][:, :LANES], _bd(e1[i])) for i in n]
    dm = [_dot(d1[i], cat([_bd(jnp.where(m2, n_ab[i], 0.0)), _bd(at[i]), _bd(akv[i])], axis=1))
          for i in n]
    p2 = [x[:, :LANES] for x in dm]
    z = [x[:, LANES:] for x in dm]
    qm = [_dot(p2[i], cat([_bd(p2[i]), _bd2(z[i])], axis=1)) for i in n]
    z = [z[i] + qm[i][:, LANES:] for i in n]
    z = [z[i] + _dot(qm[i][:, :LANES], _bd2(z[i])) for i in n]

    top = [_dot(aa[i][L:, :], cat([_bd2(z[i]), cat([zero_sq, _bd(vv[i])], axis=1)], axis=0))
           for i in n]
    gm = [g_mid[c[0]][:, c[1] * LANES:(c[1] + 1) * LANES] for c in cs]
    ge = [g_end[c[0]][:, c[1] * LANES:(c[1] + 1) * LANES] for c in cs]
    q_eff = [(rt[i] + top[i][:, :LANES]) * gm[i] for i in n]
    m_low = [_tn_pair(bh[i], z[i][:, :LANES]) * gm[i] for i in n]
    g_t = [_tn_pair(cat([z[i][:, LANES:], vv[i]], axis=0), cat([bh[i], kh[i]], axis=0)) for i in n]

    state = [ht_ref[pr] for pr in range(HEADS // 2)]
    y_tiles = []
    for i, (s, pr) in enumerate(cs):
        y_tiles.append(_dot_nt(q_eff[i], _bd(state[pr])) + top[i][:, LANES:])
        state[pr] = state[pr] * ge[i] + _dot_nt(state[pr], _bd(m_low[i])) + g_t[i]
    for pr in range(HEADS // 2):
        ht_ref[pr] = state[pr]

    npr = HEADS // 2
    y = cat([cat(y_tiles[s * npr:(s + 1) * npr], axis=1) for s in range(nsub)], axis=0)
    inv_n = 1.0 / HEAD
    yc = y - _seg_sum(y, ones_bd) * inv_n
    emit_proj(1)
    var = _seg_sum(yc * yc, ones_bd) * inv_n
    y_n = yc * lax.rsqrt(var + TM_LN_EPS) * lng_ref[...] + lnb_ref[...]
    emit_proj(1)
    bonus = _seg_sum(r * k2 * rk_ref[...], ones_bd) * v
    y_ref[:, :W] = ((y_n + bonus) * _silu(g)).astype(BF16)
    emit_proj()


def _even_layer(x, scale1, shift, gate, norm_pre, w_in, norm_post, w_out, mu, w0, w2, a0, a2, k_k, k_a, r_k,
                lnx_g, lnx_b, conv_w):
    row = lambda a: a.reshape(1, -1)
    params = [row(mu), row(w0), w2, row(a0), a2, row(k_k), row(k_a), row(r_k), row(lnx_g), row(lnx_b), conv_w]
    scratch = [
        pltpu.VMEM((8, TM_COLS), F32),
        pltpu.VMEM((HEADS // 2, HEAD, LANES), F32),
        pltpu.VMEM((8, WIDTH), F32),
    ]
    return _layer_call(_even_layer_kernel, "even_layer", EVEN_TILE, x, scale1, shift, gate, norm_pre, w_in,
                       norm_post, w_out, params, scratch)


def _odd_layer_kernel(xn_ref, xr_ref, sc_ref, sh_ref, gate_ref, npre_ref, win_ref, npost_ref, wout_ref,
                      cfw_ref, cfb_ref, cfg_ref, cfbb_ref, scw_ref, scb_ref, dtb_ref, alog_ref, dsk_ref, ng_ref,
                      o_ref, pbuf_ref, y_ref, ubuf_ref, xc_ref, st_ref, *, n_tiles):
    L = CHUNK
    T = xn_ref.shape[0]
    nsub = T // L
    W = WIDTH
    cat = jnp.concatenate
    step = pl.program_id(0)

    @pl.when(step == 0)
    def _():
        pbuf_ref[...] = jnp.zeros_like(pbuf_ref)
        y_ref[...] = jnp.zeros_like(y_ref)

    @pl.when((step == 0) | ((step + n_tiles - 1) % n_tiles == 0))
    def _():
        ubuf_ref[...] = jnp.zeros_like(ubuf_ref)
        xc_ref[...] = jnp.zeros_like(xc_ref)
        st_ref[...] = jnp.zeros_like(st_ref)

    _residual_update(y_ref, xr_ref, wout_ref, npost_ref, gate_ref, o_ref)

    p_ref = pbuf_ref
    val = p_ref[:, 0:W]
    glu = p_ref[:, W:2 * W]
    g = p_ref[:, 2 * W:3 * W]
    o = 3 * W
    z = p_ref[:, o:o + W]
    xbc_in = p_ref[:, o + W:o + W + SSD_XBC]
    dt_in = p_ref[:, o + W + SSD_XBC:o + W + SSD_XBC + LANES]
    emit_proj = _projection_emitter(xn_ref, npre_ref, sc_ref, sh_ref, win_ref, pbuf_ref, block=4 * LANES)

    ubuf_ref[CF_HALO:CF_HALO + T, :] = val * _sigmoid(glu)
    first = CF_HALO - (CF_KERNEL - 1)
    acc = jnp.zeros((T, W), F32) + cfb_ref[...]
    proj_blk = None
    for rr in range(8):
        part = None
        for i in range(CF_KERNEL):
            off = first + i
            if off % 8 != rr:
                continue
            w_row = cfw_ref[i:i + 1, :]
            if part is None and proj_blk is not None:
                w_row = w_row + _zero_after(proj_blk, W)
            term = w_row * ubuf_ref[off - rr:off - rr + T + 8, :]
            part = term if part is None else part + term
        if rr == 0:
            acc = acc + part[:T]
        else:
            acc = acc + pltpu.roll(part, T + 8 - rr, 0)[:T]
        proj_blk = emit_proj(1)
    ubuf_ref[:CF_HALO, :] = ubuf_ref[T:T + CF_HALO, :]
    mean = jnp.mean(acc, axis=-1, keepdims=True)
    ac = acc - mean
    var = jnp.mean(ac * ac, axis=-1, keepdims=True)
    ln = ac * lax.rsqrt(var + CF_LN_EPS) * cfg_ref[...] + cfbb_ref[...]
    y_ref[:, :W] = (_silu(ln) * _silu(g)).astype(BF16)
    emit_proj(1)

    carry = xc_ref[...]
    conv = (scw_ref[0:1, :] * _shift_rows(xbc_in, carry, 3) + scw_ref[1:2, :] * _shift_rows(xbc_in, carry, 2) +
            scw_ref[2:3, :] * _shift_rows(xbc_in, carry, 1) + scw_ref[3:4, :] * xbc_in + scb_ref[...])
    xc_ref[...] = xbc_in[T - 8:T, :]
    emit_proj(1)
    xbc = _silu(conv)
    emit_proj(1)
    xs = xbc[:, :W]
    dt = _softplus(dt_in + dtb_ref[...])
    da = dt * (-jnp.exp(alog_ref[...]))
    cs = _chunk_cumsum(da)
    expand = ((lax.broadcasted_iota(jnp.int32, (LANES, W), 1) >> 6) ==
              lax.broadcasted_iota(jnp.int32, (LANES, W), 0)).astype(BF16)

    def widen(x):
        hi, lo = _split_hi_lo(x)
        return (jnp.dot(hi, expand, preferred_element_type=F32) + jnp.dot(lo, expand, preferred_element_type=F32))

    cs_w = widen(cs)
    x_dt = xs * widen(dt)
    ri = lax.broadcasted_iota(jnp.int32, (L, W), 0)
    cj = lax.broadcasted_iota(jnp.int32, (L, W), 1) & (HEAD - 1)
    diag = (ri == cj).astype(F32)
    causal = ri >= cj
    ones_ll = jnp.ones((L, L), F32)

    y_chunks = []
    state = [st_ref[pr] for pr in range(HEADS // 2)]
    for s in range(nsub):
        rows = slice(s * L, (s + 1) * L)
        csw = cs_w[rows]
        last = csw[L - 1:L, :]
        row_b = _dot_exact_lhs(ones_ll, csw * diag)
        lmat = jnp.exp(jnp.where(causal, csw - row_b, -1e30))
        xd = x_dt[rows]
        xd_dec = xd * jnp.exp(last - csw)
        e_cs = jnp.exp(csw)
        e_last = jnp.exp(last)
        tiles = []
        for grp in range(SSD_GROUPS):
            bm = xbc[rows, W + grp * SSD_STATE:W + (grp + 1) * SSD_STATE]
            cm = xbc[rows, W + (SSD_GROUPS + grp) * SSD_STATE:W + (SSD_GROUPS + grp + 1) * SSD_STATE]
            cb2 = _dot_nt(cm, cat([bm, bm], axis=0))
            for e in range(SSD_HPG // 2):
                pr = grp * (SSD_HPG // 2) + e
                lanes = slice(pr * LANES, (pr + 1) * LANES)
                y_diag = _dot(cb2 * lmat[:, lanes], _bd(xd[:, lanes]))
                y_off = _dot(cm, state[pr]) * e_cs[:, lanes]
                state[pr] = state[pr] * e_last[:, lanes] + _dot_tn(bm, xd_dec[:, lanes])
                tiles.append(y_diag + y_off)
        y_chunks.append(cat(tiles, axis=1))
    for pr in range(HEADS // 2):
        st_ref[pr] = state[pr]
    y = cat(y_chunks, axis=0) + xs * dsk_ref[...]
    y = y * _silu(z)
    ms = jnp.mean(y * y, axis=-1, keepdims=True)
    y_ref[:, W:] = (y * lax.rsqrt(ms + NORM_EPS) * ng_ref[...]).astype(BF16)
    emit_proj()


def _odd_layer(x, scale1, shift, gate, norm_pre, w_in, norm_post, w_out, cf_w, cf_b, cf_g, cf_bb, sc_w, sc_b,
               dt_bias, a_log, d_skip, norm_g):
    row = lambda a: a.reshape(1, -1)
    pad_lanes = lambda a: jnp.zeros((1, LANES), F32).at[0, :a.shape[0]].set(a)
    params = [cf_w, row(cf_b), row(cf_g), row(cf_bb), sc_w, row(sc_b), pad_lanes(dt_bias), pad_lanes(a_log),
              row(jnp.repeat(d_skip, HEAD)), row(norm_g)]
    scratch = [
        pltpu.VMEM((CF_HALO + ODD_TILE + 8, WIDTH), F32),
        pltpu.VMEM((8, SSD_XBC), F32),
        pltpu.VMEM((HEADS // 2, SSD_STATE, LANES), F32),
    ]
    return _layer_call(_odd_layer_kernel, "odd_layer", ODD_TILE, x, scale1, shift, gate, norm_pre, w_in,
                       norm_post, w_out, params, scratch)


def kernel(x, c, ada_w, ada_b, norm_pre, norm_post, ev_w_in, ev_w_out, tm_mu, tm_w0, tm_w2, tm_a0, tm_a2,
           tm_k_k, tm_k_a, tm_r_k, tm_lnx_g, tm_lnx_b, sc_conv_w, od_w_in, od_w_out, cf_conv_w, cf_conv_b,
           cf_ln_g, cf_ln_b, ssd_conv_w, ssd_conv_b, ssd_dt_bias, ssd_a_log, ssd_d, ssd_norm_g):
    depth = ada_w.shape[0]
    bsz, t, d = x.shape
    assert d == D_MODEL and t % EVEN_TILE == 0 and t % ODD_TILE == 0
    mod = _ada_modulation(c, ada_w, ada_b)
    od_cols = od_w_in.shape[-1]
    od_w_pad = jnp.zeros(od_w_in.shape[:2] + (ODD_COLS_PAD,), BF16).at[..., :od_cols].set(od_w_in.astype(BF16))
    for i in range(depth):
        shift = mod[i, :, None, 0:d]
        scale1 = 1.0 + mod[i, :, None, d:2 * d]
        gate = mod[i, :, None, 2 * d:3 * d]
        j = i // 2
        if i % 2 == 0:
            x = _even_layer(x, scale1, shift, gate, norm_pre[i], ev_w_in[j].astype(BF16), norm_post[i],
                            ev_w_out[j].astype(BF16), tm_mu[j], tm_w0[j], tm_w2[j], tm_a0[j], tm_a2[j],
                            tm_k_k[j].reshape(-1), tm_k_a[j].reshape(-1), tm_r_k[j].reshape(-1), tm_lnx_g[j],
                            tm_lnx_b[j], sc_conv_w[j])
        else:
            x = _odd_layer(x, scale1, shift, gate, norm_pre[i], od_w_pad[j], norm_post[i],
                           od_w_out[j].astype(BF16), cf_conv_w[j], cf_conv_b[j], cf_ln_g[j], cf_ln_b[j],
                           ssd_conv_w[j], ssd_conv_b[j], ssd_dt_bias[j], ssd_a_log[j], ssd_d[j], ssd_norm_g[j])
    return x
```

```python
import functools

import jax
import jax.numpy as jnp
from jax import lax
from jax.experimental import pallas as pl
from jax.experimental.pallas import tpu as pltpu

F32 = jnp.float32
BF16 = jnp.bfloat16

D_MODEL = 1024
CHUNK = 64
EVEN_TILE = 512
ODD_TILE = 512
HEAD = 64
HEADS = 8
WIDTH = 512
LORA = 64
TM_COLS = 4 * WIDTH + 2 * LORA
EVEN_COLS = TM_COLS + 4 * WIDTH
SSD_STATE = 128
SSD_GROUPS = 2
SSD_HPG = HEADS // SSD_GROUPS
SSD_XBC = WIDTH + 2 * SSD_GROUPS * SSD_STATE
CF_KERNEL = 31
CF_HALO = 32
LANES = 128
ODD_COLS_PAD = 3 * WIDTH + WIDTH + SSD_XBC + LANES
NORM_EPS = 1e-6
TM_LN_EPS = 64e-5
CF_LN_EPS = 1e-5
VMEM_LIMIT = 56 * 1024 * 1024


def _dot(a, b):
    return jnp.dot(a.astype(BF16), b.astype(BF16), preferred_element_type=F32)


def _dot_nt(a, b):
    return lax.dot_general(a.astype(BF16), b.astype(BF16), (((1,), (1,)), ((), ())),
                           preferred_element_type=F32)


def _dot_tn(a, b):
    return lax.dot_general(a.astype(BF16), b.astype(BF16), (((0,), (0,)), ((), ())),
                           preferred_element_type=F32)


def _split_hi_lo(x):
    hi = x.astype(BF16)
    lo = (x - hi.astype(F32)).astype(BF16)
    return hi, lo


def _dot_exact_lhs(m, x):
    hi, lo = _split_hi_lo(x)
    mb = m.astype(BF16)
    return (jnp.dot(mb, hi, preferred_element_type=F32) +
            jnp.dot(mb, lo, preferred_element_type=F32))


def _chunk_cumsum(x):
    ii = lax.broadcasted_iota(jnp.int32, (CHUNK, CHUNK), 0)
    jj = lax.broadcasted_iota(jnp.int32, (CHUNK, CHUNK), 1)
    tri = (ii >= jj).astype(BF16)
    hi, lo = _split_hi_lo(x)
    blocks = [jnp.dot(tri, hi[s:s + CHUNK], preferred_element_type=F32) +
              jnp.dot(tri, lo[s:s + CHUNK], preferred_element_type=F32) for s in range(0, x.shape[0], CHUNK)]
    return jnp.concatenate(blocks, axis=0)


def _sigmoid(x):
    return 0.5 * jnp.tanh(0.5 * x) + 0.5


def _silu(x):
    h = 0.5 * x
    return h + h * jnp.tanh(h)


def _softplus(x):
    return jnp.maximum(x, 0.0) + jnp.log(1.0 + jnp.exp(-jnp.abs(x)))


def _shift_rows(x, carry, d):
    out = pltpu.roll(x, d, 0)
    row = lax.broadcasted_iota(jnp.int32, (8, 1), 0)
    head = jnp.where(row < d, pltpu.roll(carry, d, 0), out[:8])
    return jnp.concatenate([head, out[8:]], axis=0)


def _ada_kernel(c_ref, w_ref, b_ref, o_ref):
    hi, lo = _split_hi_lo(_silu(c_ref[...]))
    wb = w_ref[...].astype(BF16)
    o_ref[...] = (jnp.dot(hi, wb, preferred_element_type=F32) + jnp.dot(lo, wb, preferred_element_type=F32) +
                  b_ref[...])


def _ada_modulation(c, ada_w, ada_b):
    depth, d, d3 = ada_w.shape
    bsz = c.shape[0]
    rows = 16
    c_pad = jnp.zeros((rows, d), F32).at[:bsz].set(c)
    tn = 1536
    out = pl.pallas_call(
        _ada_kernel,
        grid=(depth, d3 // tn),
        in_specs=[
            pl.BlockSpec((rows, d), lambda i, j: (0, 0)),
            pl.BlockSpec((None, d, tn), lambda i, j: (i, 0, j)),
            pl.BlockSpec((None, 1, tn), lambda i, j: (i, 0, j)),
        ],
        out_specs=pl.BlockSpec((None, rows, tn), lambda i, j: (i, 0, j)),
        out_shape=jax.ShapeDtypeStruct((depth, rows, d3), F32),
        compiler_params=pltpu.CompilerParams(
            dimension_semantics=("parallel", "parallel"), vmem_limit_bytes=VMEM_LIMIT),
        name="ada_modulation",
    )(c_pad, ada_w, ada_b.reshape(depth, 1, d3))
    return out[:, :bsz]


PIPE_LAG = 2


PROJ_BLOCK = 3 * LANES


def _projection_emitter(xn_ref, npre_ref, sc_ref, sh_ref, win_ref, p_ref, block=PROJ_BLOCK):
    x = xn_ref[...]
    ms = jnp.mean(x * x, axis=-1, keepdims=True)
    h = x * lax.rsqrt(ms + NORM_EPS) * npre_ref[...]
    h = (h * sc_ref[...] + sh_ref[...]).astype(BF16)
    cols = win_ref.shape[1]
    starts = list(range(0, cols, block))

    def emit(k=None):
        blk = None
        for _ in range(len(starts) if k is None else min(k, len(starts))):
            lo = starts.pop(0)
            hi = min(lo + block, cols)
            blk = jnp.dot(h, win_ref[:, lo:hi], preferred_element_type=F32)
            p_ref[:, lo:hi] = blk
        return blk

    return emit


def _zero_after(dep, width):
    bits = lax.bitcast_convert_type(dep[0:8, 0:LANES], jnp.int32)
    zero = lax.shift_right_logical(lax.shift_right_logical(bits, 16), 16).astype(F32)
    return jnp.concatenate([zero[0:1, :]] * (width // LANES), axis=1)


def _residual_update(y_ref, x_ref, wout_ref, npost_ref, gate_ref, o_ref):
    yo = jnp.dot(y_ref[...], wout_ref[...], preferred_element_type=F32)
    ms = jnp.mean(yo * yo, axis=-1, keepdims=True)
    o_ref[...] = x_ref[...] + gate_ref[...] * (yo * lax.rsqrt(ms + NORM_EPS) * npost_ref[...])


def _layer_call(body, name, tile, x, scale1, shift, gate, norm_pre, w_in, norm_post, w_out, params, scratch):
    bsz, t, d = x.shape
    n_tiles = t // tile
    last = bsz * n_tiles - 1
    cols = w_in.shape[1]
    row = lambda a: a.reshape(1, -1)
    full = lambda a: pl.BlockSpec(a.shape, lambda s: (0,) * a.ndim)
    tile_in = lambda s: jnp.minimum(s, last)
    tile_out = lambda s: jnp.maximum(s - PIPE_LAG, 0)
    xn_spec = pl.BlockSpec((tile, d), lambda s: (tile_in(s), 0))
    xr_spec = pl.BlockSpec((tile, d), lambda s: (tile_out(s), 0))
    mod_in = pl.BlockSpec((None, 1, d), lambda s: (tile_in(s) // n_tiles, 0, 0))
    mod_out = pl.BlockSpec((None, 1, d), lambda s: (tile_out(s) // n_tiles, 0, 0))
    shared = [row(norm_pre), w_in, row(norm_post), w_out] + list(params)
    x2 = x.reshape(bsz * t, d)
    out = pl.pallas_call(
        functools.partial(body, n_tiles=n_tiles),
        grid=(bsz * n_tiles + PIPE_LAG,),
        in_specs=[xn_spec, xr_spec, mod_in, mod_in, mod_out] + [full(a) for a in shared],
        out_specs=xr_spec,
        out_shape=jax.ShapeDtypeStruct((bsz * t, d), F32),
        scratch_shapes=[
            pltpu.VMEM((tile, cols), F32),
            pltpu.VMEM((tile, 2 * WIDTH), BF16),
        ] + list(scratch),
        compiler_params=pltpu.CompilerParams(
            dimension_semantics=("arbitrary",), vmem_limit_bytes=VMEM_LIMIT),
        name=name,
    )(x2, x2, scale1, shift, gate, *shared)
    return out.reshape(bsz, t, d)


def _bd(x):
    lane = lax.broadcasted_iota(jnp.int32, x.shape, 1)
    return jnp.concatenate([jnp.where(lane < HEAD, x, 0.0), jnp.where(lane >= HEAD, x, 0.0)],
                           axis=0).astype(BF16)


def _bd2(z):
    return jnp.concatenate([_bd(z[:, :LANES]), _bd(z[:, LANES:])], axis=1)


def _tn_pair(x, y):
    full = _dot_tn(x, y)
    lane = lax.broadcasted_iota(jnp.int32, (HEAD, LANES), 1)
    return jnp.where(lane < HEAD, full[:HEAD], full[HEAD:])


def _seg_sum(x, ones_bd):
    rows = x.shape[0]
    n = x.shape[1] // LANES
    xs = jnp.concatenate([x[:, j * LANES:(j + 1) * LANES] for j in range(n)], axis=0)
    s = jnp.dot(xs.astype(BF16), ones_bd, preferred_element_type=F32)
    return jnp.concatenate([s[j * rows:(j + 1) * rows] for j in range(n)], axis=1)


def _even_layer_kernel(xn_ref, xr_ref, sc_ref, sh_ref, gate_ref, npre_ref, win_ref, npost_ref, wout_ref,
                       mu_ref, w0_ref, w2_ref, a0_ref, a2_ref, kk_ref, ka_ref, rk_ref, lng_ref, lnb_ref, cw_ref,
                       o_ref, pbuf_ref, y_ref, prow_ref, ht_ref, uc_ref, *, n_tiles):
    L = CHUNK
    T = xn_ref.shape[0]
    nsub = T // L
    W = WIDTH
    cat = jnp.concatenate
    step = pl.program_id(0)

    @pl.when(step == 0)
    def _():
        pbuf_ref[...] = jnp.zeros_like(pbuf_ref)
        y_ref[...] = jnp.zeros_like(y_ref)

    @pl.when((step == 0) | ((step + n_tiles - 1) % n_tiles == 0))
    def _():
        prow_ref[...] = jnp.zeros_like(prow_ref)
        ht_ref[...] = jnp.zeros_like(ht_ref)
        uc_ref[...] = jnp.zeros_like(uc_ref)

    _residual_update(y_ref, xr_ref, wout_ref, npost_ref, gate_ref, o_ref)

    p_ref = pbuf_ref
    p_tm = p_ref[:, :TM_COLS]
    o = TM_COLS
    b_gate = p_ref[:, o:o + W]
    c_gate = p_ref[:, o + W:o + 2 * W]
    hh = p_ref[:, o + 2 * W:o + 3 * W]
    g2 = p_ref[:, o + 3 * W:o + 4 * W]
    emit_proj = _projection_emitter(xn_ref, npre_ref, sc_ref, sh_ref, win_ref, pbuf_ref, block=4 * LANES)

    blk = emit_proj(1)

    u = c_gate * hh
    carry = uc_ref[...]
    conv = (cw_ref[0:1, :] * _shift_rows(u, carry, 2) + cw_ref[1:2, :] * _shift_rows(u, carry, 1) +
            cw_ref[2:3, :] * u)
    uc_ref[...] = u[T - 8:T, :]
    y_ref[:, W:] = (b_gate * conv * _silu(g2)).astype(BF16)

    prev = _shift_rows(p_tm, prow_ref[...], 1)
    prow_ref[...] = p_tm[T - 8:T, :]
    ps = p_tm + (prev - p_tm) * (mu_ref[...] + _zero_after(blk, TM_COLS))
    blk = emit_proj(1)
    r = ps[:, 0:W]
    k = ps[:, W:2 * W]
    v = ps[:, 2 * W:3 * W]
    g = ps[:, 3 * W:4 * W]
    wd = ps[:, 4 * W:4 * W + LORA]
    ad = ps[:, 4 * W + LORA:4 * W + 2 * LORA]
    w_log = -_softplus(-(w0_ref[...] + _zero_after(blk, W) + _dot(jnp.tanh(wd), w2_ref[...]))) - 0.5
    lw = -jnp.exp(w_log)
    blk = emit_proj(1)
    a_icl = _sigmoid(a0_ref[...] + _zero_after(blk, W) + _dot(ad, a2_ref[...]))
    k2 = k * (1.0 + (a_icl - 1.0) * ka_ref[...])
    ones_bd = ((lax.broadcasted_iota(jnp.int32, (LANES, LANES), 0) >> 6) ==
               (lax.broadcasted_iota(jnp.int32, (LANES, LANES), 1) >> 6)).astype(BF16)
    kk = k * kk_ref[...]
    kk = kk * lax.rsqrt(jnp.maximum(_seg_sum(kk * kk, ones_bd), 1e-24))
    kb = kk * a_icl
    blk = emit_proj(2)

    cum_incl = _chunk_cumsum(lw)
    cum_excl = cum_incl - lw
    rows_of = lambda x, i: x[i:i + 1, :]
    mid_rows = [rows_of(cum_incl, s * L + L // 2 - 1) for s in range(nsub)]
    end_rows = [rows_of(cum_incl, s * L + L - 1) for s in range(nsub)]
    mid = cat([jnp.broadcast_to(m, (L, W)) for m in mid_rows], axis=0)
    end = cat([jnp.broadcast_to(m, (L, W)) for m in end_rows], axis=0)
    g_mid = [jnp.exp(m) for m in mid_rows]
    g_end = [jnp.exp(m) for m in end_rows]
    e_k = jnp.exp(mid - cum_incl)
    e_end = jnp.exp(end - cum_incl)
    blk = emit_proj(2)
    at_all = -kk * jnp.exp(cum_excl - mid)
    rt_all = r * jnp.exp(cum_incl - mid)
    bt_all = kb * e_k
    kt_all = k2 * e_k
    bh_all = kb * e_end
    kh_all = k2 * e_end
    emit_proj()

    ri = lax.broadcasted_iota(jnp.int32, (L, LANES), 0)
    cj = lax.broadcasted_iota(jnp.int32, (L, LANES), 1) & (HEAD - 1)
    eye = (ri == cj).astype(F32)
    strict = ri > cj
    same4 = (ri >> 2) == (cj >> 2)
    same16 = (ri >> 4) == (cj >> 4)
    m0 = strict & same4
    m1 = strict & same16 & jnp.logical_not(same4)
    m2 = strict & jnp.logical_not(same16)
    r2 = lax.broadcasted_iota(jnp.int32, (2 * L, 2 * LANES), 0)
    c2 = lax.broadcasted_iota(jnp.int32, (2 * L, 2 * LANES), 1) & (HEAD - 1)
    mask_aa = (r2 & (L - 1)) >= jnp.where(r2 < L, c2 + 1, c2)
    zero_sq = jnp.zeros((LANES, LANES), BF16)

    cs = [(s, pr) for s in range(nsub) for pr in range(HEADS // 2)]
    tile = lambda x, c: x[c[0] * L:(c[0] + 1) * L, c[1] * LANES:(c[1] + 1) * LANES]
    at = [tile(at_all, c) for c in cs]
    rt = [tile(rt_all, c) for c in cs]
    vv = [tile(v, c) for c in cs]
    bh = [tile(bh_all, c) for c in cs]
    kh = [tile(kh_all, c) for c in cs]
    n = range(len(cs))

    aa = [_dot_nt(cat([at[i], rt[i]], axis=0), cat([_bd(tile(bt_all, c)), _bd(tile(kt_all, c))], axis=0))
          for i, c in enumerate(cs)]
    aa = [jnp.where(mask_aa, x, 0.0) for x in aa]
    n_ab = [x[:L, :LANES] for x in aa]
    akv = [_dot(aa[i][:L, LANES:], _bd(vv[i])) for i in n]

    n0 = [jnp.where(m0, x, 0.0) for x in n_ab]
    n0sq = [_dot(x, _bd(x)) for x in n0]
    d0 = [eye + n0[i] + n0sq[i] + _dot(n0[i], _bd(n0sq[i])) for i in n]
    p1 = [_dot(d0[i], _bd(jnp.where(m1, n_ab[i], 0.0))) for i in n]
    pm = [_dot(p1[i], cat([_bd(p1[i]), _bd(d0[i])], axis=1)) for i in n]
    e1 = [d0[i] + pm[i][:, LANES:] for i in n]
    d1 = [e1[i] + _dot(pm[i][:, :LANES], _bd(e1[i])) for i in n]
    dm = [_dot(d1[i], cat([_bd(jnp.where(m2, n_ab[i], 0.0)), _bd(at[i]), _bd(akv[i])], axis=1))
          for i in n]
    p2 = [x[:, :LANES] for x in dm]
    z = [x[:, LANES:] for x in dm]
    qm = [_dot(p2[i], cat([_bd(p2[i]), _bd2(z[i])], axis=1)) for i in n]
    z = [z[i] + qm[i][:, LANES:] for i in n]
    z = [z[i] + _dot(qm[i][:, :LANES], _bd2(z[i])) for i in n]

    top = [_dot(aa[i][L:, :], cat([_bd2(z[i]), cat([zero_sq, _bd(vv[i])], axis=1)], axis=0))
           for i in n]
    gm = [g_mid[c[0]][:, c[1] * LANES:(c[1] + 1) * LANES] for c in cs]
    ge = [g_end[c[0]][:, c[1] * LANES:(c[1] + 1) * LANES] for c in cs]
    q_eff = [(rt[i] + top[i][:, :LANES]) * gm[i] for i in n]
    m_low = [_tn_pair(bh[i], z[i][:, :LANES]) * gm[i] for i in n]
    g_t = [_tn_pair(cat([z[i][:, LANES:], vv[i]], axis=0), cat([bh[i], kh[i]], axis=0)) for i in n]

    state = [ht_ref[pr] for pr in range(HEADS // 2)]
    y_tiles = []
    for i, (s, pr) in enumerate(cs):
        y_tiles.append(_dot_nt(q_eff[i], _bd(state[pr])) + top[i][:, LANES:])
        state[pr] = state[pr] * ge[i] + _dot_nt(state[pr], _bd(m_low[i])) + g_t[i]
    for pr in range(HEADS // 2):
        ht_ref[pr] = state[pr]

    npr = HEADS // 2
    y = cat([cat(y_tiles[s * npr:(s + 1) * npr], axis=1) for s in range(nsub)], axis=0)
    inv_n = 1.0 / HEAD
    yc = y - _seg_sum(y, ones_bd) * inv_n
    emit_proj(1)
    var = _seg_sum(yc * yc, ones_bd) * inv_n
    y_n = yc * lax.rsqrt(var + TM_LN_EPS) * lng_ref[...] + lnb_ref[...]
    emit_proj(1)
    bonus = _seg_sum(r * k2 * rk_ref[...], ones_bd) * v
    y_ref[:, :W] = ((y_n + bonus) * _silu(g)).astype(BF16)
    emit_proj()


def _even_layer(x, scale1, shift, gate, norm_pre, w_in, norm_post, w_out, mu, w0, w2, a0, a2, k_k, k_a, r_k,
                lnx_g, lnx_b, conv_w):
    row = lambda a: a.reshape(1, -1)
    params = [row(mu), row(w0), w2, row(a0), a2, row(k_k), row(k_a), row(r_k), row(lnx_g), row(lnx_b), conv_w]
    scratch = [
        pltpu.VMEM((8, TM_COLS), F32),
        pltpu.VMEM((HEADS // 2, HEAD, LANES), F32),
        pltpu.VMEM((8, WIDTH), F32),
    ]
    return _layer_call(_even_layer_kernel, "even_layer", EVEN_TILE, x, scale1, shift, gate, norm_pre, w_in,
                       norm_post, w_out, params, scratch)


def _odd_layer_kernel(xn_ref, xr_ref, sc_ref, sh_ref, gate_ref, npre_ref, win_ref, npost_ref, wout_ref,
                      cfw_ref, cfb_ref, cfg_ref, cfbb_ref, scw_ref, scb_ref, dtb_ref, alog_ref, dsk_ref, ng_ref,
                      o_ref, pbuf_ref, y_ref, ubuf_ref, xc_ref, st_ref, *, n_tiles):
    L = CHUNK
    T = xn_ref.shape[0]
    nsub = T // L
    W = WIDTH
    cat = jnp.concatenate
    step = pl.program_id(0)

    @pl.when(step == 0)
    def _():
        pbuf_ref[...] = jnp.zeros_like(pbuf_ref)
        y_ref[...] = jnp.zeros_like(y_ref)

    @pl.when((step == 0) | ((step + n_tiles - 1) % n_tiles == 0))
    def _():
        ubuf_ref[...] = jnp.zeros_like(ubuf_ref)
        xc_ref[...] = jnp.zeros_like(xc_ref)
        st_ref[...] = jnp.zeros_like(st_ref)

    _residual_update(y_ref, xr_ref, wout_ref, npost_ref, gate_ref, o_ref)

    p_ref = pbuf_ref
    val = p_ref[:, 0:W]
    glu = p_ref[:, W:2 * W]
    g = p_ref[:, 2 * W:3 * W]
    o = 3 * W
    z = p_ref[:, o:o + W]
    xbc_in = p_ref[:, o + W:o + W + SSD_XBC]
    dt_in = p_ref[:, o + W + SSD_XBC:o + W + SSD_XBC + LANES]
    emit_proj = _projection_emitter(xn_ref, npre_ref, sc_ref, sh_ref, win_ref, pbuf_ref, block=4 * LANES)

    ubuf_ref[CF_HALO:CF_HALO + T, :] = val * _sigmoid(glu)
    first = CF_HALO - (CF_KERNEL - 1)
    acc = jnp.zeros((T, W), F32) + cfb_ref[...]
    proj_blk = None
    for rr in range(8):
        part = None
        for i in range(CF_KERNEL):
            off = first + i
            if off % 8 != rr:
                continue
            w_row = cfw_ref[i:i + 1, :]
            if part is None and proj_blk is not None:
                w_row = w_row + _zero_after(proj_blk, W)
            term = w_row * ubuf_ref[off - rr:off - rr + T + 8, :]
            part = term if part is None else part + term
        if rr == 0:
            acc = acc + part[:T]
        else:
            acc = acc + pltpu.roll(part, T + 8 - rr, 0)[:T]
        proj_blk = emit_proj(1)
    ubuf_ref[:CF_HALO, :] = ubuf_ref[T:T + CF_HALO, :]
    mean = jnp.mean(acc, axis=-1, keepdims=True)
    ac = acc - mean
    var = jnp.mean(ac * ac, axis=-1, keepdims=True)
    ln = ac * lax.rsqrt(var + CF_LN_EPS) * cfg_ref[...] + cfbb_ref[...]
    y_ref[:, :W] = (_silu(ln) * _silu(g)).astype(BF16)
    emit_proj(1)

    carry = xc_ref[...]
    conv = (scw_ref[0:1, :] * _shift_rows(xbc_in, carry, 3) + scw_ref[1:2, :] * _shift_rows(xbc_in, carry, 2) +
            scw_ref[2:3, :] * _shift_rows(xbc_in, carry, 1) + scw_ref[3:4, :] * xbc_in + scb_ref[...])
    xc_ref[...] = xbc_in[T - 8:T, :]
    emit_proj(1)
    xbc = _silu(conv)
    emit_proj(1)
    xs = xbc[:, :W]
    dt = _softplus(dt_in + dtb_ref[...])
    da = dt * (-jnp.exp(alog_ref[...]))
    cs = _chunk_cumsum(da)
    expand = ((lax.broadcasted_iota(jnp.int32, (LANES, W), 1) >> 6) ==
              lax.broadcasted_iota(jnp.int32, (LANES, W), 0)).astype(BF16)

    def widen(x):
        hi, lo = _split_hi_lo(x)
        return (jnp.dot(hi, expand, preferred_element_type=F32) + jnp.dot(lo, expand, preferred_element_type=F32))

    cs_w = widen(cs)
    x_dt = xs * widen(dt)
    ri = lax.broadcasted_iota(jnp.int32, (L, W), 0)
    cj = lax.broadcasted_iota(jnp.int32, (L, W), 1) & (HEAD - 1)
    diag = (ri == cj).astype(F32)
    causal = ri >= cj
    ones_ll = jnp.ones((L, L), F32)

    y_chunks = []
    state = [st_ref[pr] for pr in range(HEADS // 2)]
    for s in range(nsub):
        rows = slice(s * L, (s + 1) * L)
        csw = cs_w[rows]
        last = csw[L - 1:L, :]
        row_b = _dot_exact_lhs(ones_ll, csw * diag)
        lmat = jnp.exp(jnp.where(causal, csw - row_b, -1e30))
        xd = x_dt[rows]
        xd_dec = xd * jnp.exp(last - csw)
        e_cs = jnp.exp(csw)
        e_last = jnp.exp(last)
        tiles = []
        for grp in range(SSD_GROUPS):
            bm = xbc[rows, W + grp * SSD_STATE:W + (grp + 1) * SSD_STATE]
            cm = xbc[rows, W + (SSD_GROUPS + grp) * SSD_STATE:W + (SSD_GROUPS + grp + 1) * SSD_STATE]
            cb2 = _dot_nt(cm, cat([bm, bm], axis=0))
            for e in range(SSD_HPG // 2):
                pr = grp * (SSD_HPG // 2) + e
                lanes = slice(pr * LANES, (pr + 1) * LANES)
                y_diag = _dot(cb2 * lmat[:, lanes], _bd(xd[:, lanes]))
                y_off = _dot(cm, state[pr]) * e_cs[:, lanes]
                state[pr] = state[pr] * e_last[:, lanes] + _dot_tn(bm, xd_dec[:, lanes])
                tiles.append(y_diag + y_off)
        y_chunks.append(cat(tiles, axis=1))
    for pr in range(HEADS // 2):
        st_ref[pr] = state[pr]
    y = cat(y_chunks, axis=0) + xs * dsk_ref[...]
    y = y * _silu(z)
    ms = jnp.mean(y * y, axis=-1, keepdims=True)
    y_ref[:, W:] = (y * lax.rsqrt(ms + NORM_EPS) * ng_ref[...]).astype(BF16)
    emit_proj()


def _odd_layer(x, scale1, shift, gate, norm_pre, w_in, norm_post, w_out, cf_w, cf_b, cf_g, cf_bb, sc_w, sc_b,
               dt_bias, a_log, d_skip, norm_g):
    row = lambda a: a.reshape(1, -1)
    pad_lanes = lambda a: jnp.zeros((1, LANES), F32).at[0, :a.shape[0]].set(a)
    params = [cf_w, row(cf_b), row(cf_g), row(cf_bb), sc_w, row(sc_b), pad_lanes(dt_bias), pad_lanes(a_log),
              row(jnp.repeat(d_skip, HEAD)), row(norm_g)]
    scratch = [
        pltpu.VMEM((CF_HALO + ODD_TILE + 8, WIDTH), F32),
        pltpu.VMEM((8, SSD_XBC), F32),
        pltpu.VMEM((HEADS // 2, SSD_STATE, LANES), F32),
    ]
    return _layer_call(_odd_layer_kernel, "odd_layer", ODD_TILE, x, scale1, shift, gate, norm_pre, w_in,
                       norm_post, w_out, params, scratch)


def kernel(x, c, ada_w, ada_b, norm_pre, norm_post, ev_w_in, ev_w_out, tm_mu, tm_w0, tm_w2, tm_a0, tm_a2,
           tm_k_k, tm_k_a, tm_r_k, tm_lnx_g, tm_lnx_b, sc_conv_w, od_w_in, od_w_out, cf_conv_w, cf_conv_b,
           cf_ln_g, cf_ln_b, ssd_conv_w, ssd_conv_b, ssd_dt_bias, ssd_a_log, ssd_d, ssd_norm_g):
    depth = ada_w.shape[0]
    bsz, t, d = x.shape
    assert d == D_MODEL and t % EVEN_TILE == 0 and t % ODD_TILE == 0
    mod = _ada_modulation(c, ada_w, ada_b)
    od_cols = od_w_in.shape[-1]
    od_w_pad = jnp.pad(od_w_in.astype(BF16), ((0, 0), (0, 0), (0, ODD_COLS_PAD - od_cols)))
    for i in range(depth):
        shift = mod[i, :, None, 0:d]
        scale1 = 1.0 + mod[i, :, None, d:2 * d]
        gate = mod[i, :, None, 2 * d:3 * d]
        j = i // 2
        if i % 2 == 0:
            x = _even_layer(x, scale1, shift, gate, norm_pre[i], ev_w_in[j].astype(BF16), norm_post[i],
                            ev_w_out[j].astype(BF16), tm_mu[j], tm_w0[j], tm_w2[j], tm_a0[j], tm_a2[j],
                            tm_k_k[j].reshape(-1), tm_k_a[j].reshape(-1), tm_r_k[j].reshape(-1), tm_lnx_g[j],
                            tm_lnx_b[j], sc_conv_w[j])
        else:
            x = _odd_layer(x, scale1, shift, gate, norm_pre[i], od_w_pad[j], norm_post[i],
                           od_w_out[j].astype(BF16), cf_conv_w[j], cf_conv_b[j], cf_ln_g[j], cf_ln_b[j],
                           ssd_conv_w[j], ssd_conv_b[j], ssd_dt_bias[j], ssd_a_log[j], ssd_d[j], ssd_norm_g[j])
    return x
```
